```python
import math
import jax, jax.numpy as jnp
from jax import lax
import numpy as np

D_MODEL = 1024
BATCH = 8
SEQ = 4096
DEPTH = 2

N_A_LAYERS = DEPTH // 2
N_B_LAYERS = DEPTH - N_A_LAYERS

A_HEADS = 8
A_DK = 128
A_DV = 128
A_CONV = 4
A_CHUNK = 64
A_QK = A_HEADS * A_DK
A_V = A_HEADS * A_DV
A_IN = 2 * A_QK + 2 * A_V + 2 * A_HEADS

B_HEADS = 16
B_KV_HEADS = 4
B_GROUP = B_HEADS // B_KV_HEADS
B_DH = 64
B_WINDOW = 128
B_BLOCK = 128

FFN_DIM = 2816
FFN_CONV = 3

EPS = 1e-6

kernel_name = 'yoco_deltanet_swa_sink_convffn'


def rmsnorm(x, g):
    xf = x.astype(jnp.float32)
    y = xf * lax.rsqrt(jnp.mean(xf * xf, axis=-1, keepdims=True) + EPS)
    return (y * g.astype(jnp.float32)).astype(x.dtype)


def l2norm(x):
    return x * lax.rsqrt(jnp.sum(x * x, axis=-1, keepdims=True) + EPS)


def causal_dwconv(x, w):
    K = w.shape[0]
    S = x.shape[1]
    xp = jnp.pad(x, ((0, 0), (K - 1, 0), (0, 0)))
    y = xp[:, K - 1:K - 1 + S] * w[K - 1]
    for j in range(K - 1):
        y = y + xp[:, j:j + S] * w[j]
    return y


def gated_delta_rule_chunked(q, k, v, beta, g):
    Bsz, S, H, DK = q.shape
    DV = v.shape[-1]
    C = A_CHUNK
    N = S // C

    def chunk(t):
        t = t.reshape((Bsz, N, C, H) + t.shape[3:])
        return jnp.moveaxis(t, 3, 1)

    q, k, v, beta, g = chunk(q), chunk(k), chunk(v), chunk(beta), chunk(g)
    gc = jnp.cumsum(g, axis=-1)
    idx = jnp.arange(C)
    strict = idx[:, None] > idx[None, :]
    incl = idx[:, None] >= idx[None, :]
    diff = gc[..., :, None] - gc[..., None, :]
    decay = jnp.exp(jnp.where(incl, diff, -jnp.inf))
    kb = k * beta[..., None]
    Lmat = jnp.where(strict, jnp.einsum('bhnid,bhnjd->bhnij', kb, k) * decay, 0.0)
    eye = jnp.eye(C, dtype=Lmat.dtype)
    rhs = jnp.concatenate([v * beta[..., None], kb * jnp.exp(gc)[..., None]], axis=-1)
    sol = lax.linalg.triangular_solve(eye + Lmat, rhs, left_side=True, lower=True, unit_diagonal=True)
    value, k_cumdecay = sol[..., :DV], sol[..., DV:]
    attn_intra = jnp.einsum('bhnid,bhnjd->bhnij', q, k) * decay
    q_dec = q * jnp.exp(gc)[..., None]
    k_dec = k * jnp.exp(gc[..., -1:] - gc)[..., None]
    g_last = jnp.exp(gc[..., -1])

    def step(state, xs):
        a_c, qd, kd, val, kcd, gl = xs
        v_new = val - jnp.einsum('bhcd,bhde->bhce', kcd, state)
        o = jnp.einsum('bhcd,bhde->bhce', qd, state) + jnp.einsum('bhij,bhje->bhie', a_c, v_new)
        state = state * gl[..., None, None] + jnp.einsum('bhcd,bhce->bhde', kd, v_new)
        return state, o

    xs = tuple(jnp.moveaxis(t, 2, 0) for t in (attn_intra, q_dec, k_dec, value, k_cumdecay, g_last))
    state0 = jnp.zeros((Bsz, H, DK, DV), jnp.float32)
    _, o = lax.scan(step, state0, xs)
    o = jnp.moveaxis(jnp.moveaxis(o, 0, 2), 1, 3)
    return o.reshape(Bsz, S, H, DV)


def deltanet_mixer(h, w_in, conv_w, A_log, dt_bias, onorm_g, w_out):
    Bsz, S, _ = h.shape
    f32 = jnp.float32
    p = h @ w_in
    qkv, gate, b_raw, a_raw = jnp.split(p, [2 * A_QK + A_V, 2 * A_QK + 2 * A_V, 2 * A_QK + 2 * A_V + A_HEADS], axis=-1)
    qkv = jax.nn.silu(causal_dwconv(qkv, conv_w))
    q, k, v = jnp.split(qkv, [A_QK, 2 * A_QK], axis=-1)
    q = l2norm(q.reshape(Bsz, S, A_HEADS, A_DK).astype(f32)) * (A_DK ** -0.5)
    k = l2norm(k.reshape(Bsz, S, A_HEADS, A_DK).astype(f32))
    v = v.reshape(Bsz, S, A_HEADS, A_DV).astype(f32)
    beta = jax.nn.sigmoid(b_raw.astype(f32))
    g = -jnp.exp(A_log.astype(f32)) * jax.nn.softplus(a_raw.astype(f32) + dt_bias.astype(f32))
    o = gated_delta_rule_chunked(q, k, v, beta, g)
    o = rmsnorm(o, onorm_g) * jax.nn.silu(gate.reshape(Bsz, S, A_HEADS, A_DV).astype(f32))
    return o.reshape(Bsz, S, A_V).astype(h.dtype) @ w_out


def swa_sink_attention(h, k_sh, v_sh, w_q, b_q, sinks, w_o, b_o):
    Bsz, S, _ = h.shape
    f32 = jnp.float32
    NB = S // B_BLOCK
    q = (h @ w_q + b_q).astype(f32) * (B_DH ** -0.5)
    q = q.reshape(Bsz, NB, B_BLOCK, B_KV_HEADS, B_GROUP, B_DH)

    def with_prev(t):
        t = t.astype(f32).reshape(Bsz, NB, B_BLOCK, B_KV_HEADS, B_DH)
        prev = jnp.pad(t, ((0, 0), (1, 0), (0, 0), (0, 0), (0, 0)))[:, :-1]
        return jnp.concatenate([prev, t], axis=2)

    kb, vb = with_prev(k_sh), with_prev(v_sh)
    slopes = (2.0 ** (-8.0 * jnp.arange(1, B_HEADS + 1, dtype=f32) / B_HEADS)).reshape(B_KV_HEADS, B_GROUP)
    qi = jnp.arange(B_BLOCK)[:, None]
    kj = jnp.arange(2 * B_BLOCK)[None, :]
    dist = qi + B_BLOCK - kj
    band = (dist >= 0) & (dist < B_WINDOW)
    bias = -slopes[:, :, None, None] * dist.astype(f32)
    sink = sinks.astype(f32).reshape(B_KV_HEADS, B_GROUP)[None, :, :, None]

    def block_attn(args):
        qb, kk, vv, blk = args
        s = jnp.einsum('bqkgd,bskd->bkgqs', qb, kk) + bias
        valid = band & ((blk > 0) | (kj >= B_BLOCK))
        s = jnp.where(valid, s, -jnp.inf)
        m = jnp.maximum(jnp.max(s, axis=-1), sink)
        p = jnp.exp(s - m[..., None])
        denom = jnp.sum(p, axis=-1) + jnp.exp(sink - m)
        o = jnp.einsum('bkgqs,bskd->bqkgd', p, vv)
        return o / jnp.transpose(denom, (0, 3, 1, 2))[..., None]

    xs = (jnp.moveaxis(q, 1, 0), jnp.moveaxis(kb, 1, 0), jnp.moveaxis(vb, 1, 0), jnp.arange(NB, dtype=jnp.int32))
    o = lax.map(block_attn, xs)
    o = jnp.moveaxis(o, 0, 1).reshape(Bsz, S, B_HEADS * B_DH)
    return o.astype(h.dtype) @ w_o + b_o


def conv_ffn(h, w_up, conv_w, conv_b, w_down):
    u = causal_dwconv(h @ w_up, conv_w) + conv_b
    gate, up = jnp.split(u, 2, axis=-1)
    return (jax.nn.silu(gate) * up) @ w_down


def setup_inputs(seed: int = 0) -> dict:
    key = jax.random.key(seed)
    ks = jax.random.split(key, 24)
    f32 = jnp.float32

    def nrm(k, shape, fan_in):
        return jax.random.normal(k, shape, f32) * (fan_in ** -0.5)

    def gain(k, shape):
        return 1.0 + 0.05 * jax.random.normal(k, shape, f32)

    def small(k, shape):
        return 0.02 * jax.random.normal(k, shape, f32)

    x = jax.random.normal(ks[0], (BATCH, SEQ, D_MODEL), f32)
    a_norm = gain(ks[1], (N_A_LAYERS, D_MODEL))
    a_w_in = nrm(ks[2], (N_A_LAYERS, D_MODEL, A_IN), D_MODEL)
    a_conv_w = nrm(ks[3], (N_A_LAYERS, A_CONV, 2 * A_QK + A_V), A_CONV)
    a_A_log = jnp.log(jax.random.uniform(ks[4], (N_A_LAYERS, A_HEADS), f32, 1.0, 16.0))
    dt = jnp.exp(jax.random.uniform(ks[5], (N_A_LAYERS, A_HEADS), f32, math.log(1e-3), math.log(1e-1)))
    a_dt_bias = dt + jnp.log(-jnp.expm1(-dt))
    a_onorm = gain(ks[6], (N_A_LAYERS, A_DV))
    a_w_out = nrm(ks[7], (N_A_LAYERS, A_V, D_MODEL), A_V)
    kv_norm = gain(ks[8], (D_MODEL,))
    kv_w = nrm(ks[9], (D_MODEL, 2 * B_KV_HEADS * B_DH), D_MODEL)
    kv_b = small(ks[10], (2 * B_KV_HEADS * B_DH,))
    b_norm = gain(ks[11], (N_B_LAYERS, D_MODEL))
    b_w_q = nrm(ks[12], (N_B_LAYERS, D_MODEL, B_HEADS * B_DH), D_MODEL)
    b_b_q = small(ks[13], (N_B_LAYERS, B_HEADS * B_DH))
    b_sinks = jax.random.normal(ks[14], (N_B_LAYERS, B_HEADS), f32)
    b_w_o = nrm(ks[15], (N_B_LAYERS, B_HEADS * B_DH, D_MODEL), B_HEADS * B_DH)
    b_b_o = small(ks[16], (N_B_LAYERS, D_MODEL))
    f_norm = gain(ks[17], (DEPTH, D_MODEL))
    f_w_up = nrm(ks[18], (DEPTH, D_MODEL, 2 * FFN_DIM), D_MODEL)
    f_conv_w = nrm(ks[19], (DEPTH, FFN_CONV, 2 * FFN_DIM), FFN_CONV)
    f_conv_b = small(ks[20], (DEPTH, 2 * FFN_DIM))
    f_w_down = nrm(ks[21], (DEPTH, FFN_DIM, D_MODEL), FFN_DIM)
    final_norm = gain(ks[22], (D_MODEL,))
    return {'x': x, 'a_norm': a_norm, 'a_w_in': a_w_in, 'a_conv_w': a_conv_w, 'a_A_log': a_A_log,
            'a_dt_bias': a_dt_bias, 'a_onorm': a_onorm, 'a_w_out': a_w_out,
            'kv_norm': kv_norm, 'kv_w': kv_w, 'kv_b': kv_b,
            'b_norm': b_norm, 'b_w_q': b_w_q, 'b_b_q': b_b_q, 'b_sinks': b_sinks, 'b_w_o': b_w_o, 'b_b_o': b_b_o,
            'f_norm': f_norm, 'f_w_up': f_w_up, 'f_conv_w': f_conv_w, 'f_conv_b': f_conv_b, 'f_w_down': f_w_down,
            'final_norm': final_norm}


def reference(x, a_norm, a_w_in, a_conv_w, a_A_log, a_dt_bias, a_onorm, a_w_out,
              kv_norm, kv_w, kv_b,
              b_norm, b_w_q, b_b_q, b_sinks, b_w_o, b_b_o,
              f_norm, f_w_up, f_conv_w, f_conv_b, f_w_down, final_norm):
    Bsz, S, _ = x.shape
    h = x
    k_sh = None
    v_sh = None
    for layer in range(DEPTH):
        if layer < N_A_LAYERS:
            i = layer
            h = h + deltanet_mixer(rmsnorm(h, a_norm[i]), a_w_in[i], a_conv_w[i], a_A_log[i],
                                   a_dt_bias[i], a_onorm[i], a_w_out[i])
        else:
            i = layer - N_A_LAYERS
            if i == 0:
                kv = rmsnorm(h, kv_norm) @ kv_w + kv_b
                k_sh, v_sh = jnp.split(kv, 2, axis=-1)
                k_sh = k_sh.reshape(Bsz, S, B_KV_HEADS, B_DH)
                v_sh = v_sh.reshape(Bsz, S, B_KV_HEADS, B_DH)
            h = h + swa_sink_attention(rmsnorm(h, b_norm[i]), k_sh, v_sh, b_w_q[i], b_b_q[i],
                                       b_sinks[i], b_w_o[i], b_b_o[i])
        h = h + conv_ffn(rmsnorm(h, f_norm[layer]), f_w_up[layer], f_conv_w[layer],
                         f_conv_b[layer], f_w_down[layer])
    return rmsnorm(h, final_norm)
```

```python
import functools

import jax
import jax.numpy as jnp
from jax import lax
from jax.experimental import pallas as pl
from jax.experimental.pallas import tpu as pltpu

F32 = jnp.float32
BF16 = jnp.bfloat16
EPS = 1e-6

LANES = 128
SUBLANES = 8
VMEM_LIMIT = 56 * 1024 * 1024

A_HEADS = 8
A_DK = 128
A_CONV = 4
A_CHUNK = 64

B_HEADS = 16
B_KV_HEADS = 4
B_GROUP = B_HEADS // B_KV_HEADS
B_DH = 64
B_WINDOW = 128

FFN_CONV = 3


def _params(n_axes, vmem=VMEM_LIMIT):
    return pltpu.CompilerParams(dimension_semantics=("arbitrary",) * n_axes,
                                vmem_limit_bytes=vmem)


def _const_spec(shape):
    nd = len(shape)
    return pl.BlockSpec(shape, lambda *_: (0,) * nd)


def _rms_scale(x):
    return lax.rsqrt(jnp.mean(x * x, axis=-1, keepdims=True) + EPS)


def _sigmoid(x):
    return 1.0 / (1.0 + jnp.exp(-x))


def _silu(x):
    return x * _sigmoid(x)


def _softplus(x):
    return jnp.maximum(x, 0.0) + jnp.log(1.0 + jnp.exp(-jnp.abs(x)))


def _in_proj_kernel(x_ref, g_ref, w_ref, ws_ref, p_ref, ps_ref, *, n_chunk):
    x = x_ref[...]
    hn = (x * _rms_scale(x) * g_ref[...]).astype(BF16)
    n_slabs = p_ref.shape[0]
    per = n_chunk // LANES
    for c in range(n_slabs // per):
        res = jnp.dot(hn, w_ref[:, c * n_chunk:(c + 1) * n_chunk], preferred_element_type=F32)
        for j in range(per):
            p_ref[c * per + j] = res[:, j * LANES:(j + 1) * LANES]
    ps_ref[...] = jnp.dot(hn, ws_ref[...], preferred_element_type=F32)


def _in_proj(x2, g, w_main, w_small, *, tm):
    T, D = x2.shape
    N = w_main.shape[1]
    n_slabs = N // LANES
    return pl.pallas_call(
        functools.partial(_in_proj_kernel, n_chunk=512),
        grid=(T // tm,),
        in_specs=[pl.BlockSpec((tm, D), lambda i: (i, 0)),
                  _const_spec((1, D)),
                  _const_spec((D, N)),
                  _const_spec((D, LANES))],
        out_specs=[pl.BlockSpec((n_slabs, tm, LANES), lambda i: (0, i, 0)),
                   pl.BlockSpec((tm, LANES), lambda i: (i, 0))],
        out_shape=[jax.ShapeDtypeStruct((n_slabs, T, LANES), F32),
                   jax.ShapeDtypeStruct((T, LANES), F32)],
        compiler_params=_params(1),
        name="in_proj",
    )(x2, g, w_main, w_small)


def _unit_lower_inverse(l_tiles):
    n_tiles = len(l_tiles)
    n = n_tiles * SUBLANES
    rows = lax.broadcasted_iota(jnp.int32, (SUBLANES, n), 0)
    cols = lax.broadcasted_iota(jnp.int32, (SUBLANES, n), 1)
    t_tiles = [jnp.where(cols == rows + t * SUBLANES, 1.0, 0.0).astype(F32) for t in range(n_tiles)]
    for m in range(n - 1):
        src = t_tiles[m // SUBLANES]
        row = src[m % SUBLANES:m % SUBLANES + 1, :]
        for t in range((m + 1) // SUBLANES, n_tiles):
            t_tiles[t] = t_tiles[t] - l_tiles[t][:, m:m + 1] * row
    return t_tiles


def _delta_kernel(q_ref, k_ref, v_ref, gt_ref, ps_ref, cw_ref, arow_ref, brow_ref, on_ref,
                  o_ref,
                  state_scr, halo_scr, xp_scr, qkv_scr, kt_scr, gc_scr, beta_scr, grow_scr, o_scr):
    s = pl.program_id(1)
    H, Tb, DK = q_ref.shape
    C = A_CHUNK
    n_chunks = Tb // C

    @pl.when(s == 0)
    def _():
        state_scr[...] = jnp.zeros_like(state_scr)
        halo_scr[...] = jnp.zeros_like(halo_scr)

    for sec, ref in enumerate((q_ref, k_ref, v_ref)):
        for h in range(H):
            x = ref[h]
            xp_scr[0:SUBLANES, :] = halo_scr[sec, h]
            xp_scr[SUBLANES:SUBLANES + Tb, :] = x
            halo_scr[sec, h] = x[Tb - SUBLANES:, :]
            w = cw_ref[sec, h]
            y = x * w[A_CONV - 1:A_CONV, :]
            for j in range(A_CONV - 1):
                off = SUBLANES - (A_CONV - 1) + j
                y = y + xp_scr[off:off + Tb, :] * w[j:j + 1, :]
            y = _silu(y)
            if sec < 2:
                y = y * lax.rsqrt(jnp.sum(y * y, axis=-1, keepdims=True) + EPS)
            if sec == 0:
                y = y * (DK ** -0.5)
            qkv_scr[sec, h] = y
            if sec == 1:
                yt = y.T
                for c in range(n_chunks):
                    kt_scr[h, c] = yt[:, c * C:(c + 1) * C]

    ps = ps_ref[...]
    beta_scr[...] = _sigmoid(ps)
    g = -jnp.exp(arow_ref[...]) * _softplus(ps + brow_ref[...])
    pos = lax.broadcasted_iota(jnp.int32, (Tb, LANES), 0) % C
    gc = g
    sh = 1
    while sh < C:
        gc = gc + jnp.where(pos >= sh, pltpu.roll(gc, sh, 0), 0.0)
        sh *= 2
    gc_scr[...] = gc
    gct = gc.T
    for c in range(n_chunks):
        grow_scr[c] = gct[:, c * C:(c + 1) * C]

    ri = lax.broadcasted_iota(jnp.int32, (C, C), 0)
    ci = lax.broadcasted_iota(jnp.int32, (C, C), 1)
    incl = ri >= ci
    strict = ri > ci

    def chunk_body(c, carry):
        r0 = pl.multiple_of(c * C, C)
        gc_c = gc_scr[pl.ds(r0, C), :]
        beta_c = beta_scr[pl.ds(r0, C), :]
        grow_c = grow_scr[c]
        for h in range(H):
            q = qkv_scr[0, h, pl.ds(r0, C), :]
            k = qkv_scr[1, h, pl.ds(r0, C), :]
            v = qkv_scr[2, h, pl.ds(r0, C), :]
            kt = kt_scr[h, c]
            gcol = gc_c[:, H + h:H + h + 1]
            bcol = beta_c[:, h:h + 1]
            grow = grow_c[H + h:H + h + 1, :]
            glast = grow[:, C - 1:C]

            decay = jnp.exp(jnp.where(incl, gcol - grow, -jnp.inf))
            kb = k * bcol
            kq = jnp.concatenate([kb, q], axis=0).astype(BF16)
            sc = jnp.dot(kq, kt.astype(BF16), preferred_element_type=F32)
            lmat = jnp.where(strict, sc[:C] * decay, 0.0)
            attn = sc[C:] * decay
            t_tiles = _unit_lower_inverse(
                [lmat[t * SUBLANES:(t + 1) * SUBLANES, :] for t in range(C // SUBLANES)])
            tmat = jnp.concatenate(t_tiles, axis=0).astype(BF16)

            st = state_scr[h]
            egc = jnp.exp(gcol)
            kqd = jnp.concatenate([k * egc, q * egc], axis=0).astype(BF16)
            pr = jnp.dot(kqd, st.astype(BF16), preferred_element_type=F32)
            u = bcol * (v - pr[:C])
            v_new = jnp.dot(tmat, u.astype(BF16), preferred_element_type=F32)
            v_new_b = v_new.astype(BF16)
            o = pr[C:] + jnp.dot(attn.astype(BF16), v_new_b, preferred_element_type=F32)
            kdt = (kt * jnp.exp(glast - grow)).astype(BF16)
            state_scr[h] = st * jnp.exp(glast) + jnp.dot(kdt, v_new_b, preferred_element_type=F32)
            o_scr[h, pl.ds(r0, C), :] = o
        return carry

    lax.fori_loop(0, n_chunks, chunk_body, 0)

    for h in range(H):
        o = o_scr[h]
        gate = gt_ref[h]
        y = o * _rms_scale(o) * on_ref[...] * _silu(gate)
        o_ref[:, h * DK:(h + 1) * DK] = y.astype(o_ref.dtype)


def _delta(p, ps, cw, arow, brow, onorm, *, batch, seq, tb):
    H = A_HEADS
    DK = A_DK
    T = batch * seq
    ns = seq // tb
    n_chunks = tb // A_CHUNK

    def sec_spec(sec):
        return pl.BlockSpec((H, tb, DK), lambda b, s: (sec, b * ns + s, 0))

    return pl.pallas_call(
        _delta_kernel,
        grid=(batch, ns),
        in_specs=[sec_spec(0), sec_spec(1), sec_spec(2), sec_spec(3),
                  pl.BlockSpec((tb, LANES), lambda b, s: (b * ns + s, 0)),
                  _const_spec((3, H, A_CONV, DK)),
                  _const_spec((1, LANES)), _const_spec((1, LANES)), _const_spec((1, DK))],
        out_specs=pl.BlockSpec((tb, H * DK), lambda b, s: (b * ns + s, 0)),
        out_shape=jax.ShapeDtypeStruct((T, H * DK), BF16),
        scratch_shapes=[
            pltpu.VMEM((H, DK, DK), F32),
            pltpu.VMEM((3, H, SUBLANES, DK), F32),
            pltpu.VMEM((tb + SUBLANES, DK), F32),
            pltpu.VMEM((3, H, tb, DK), F32),
            pltpu.VMEM((H, n_chunks, DK, A_CHUNK), F32),
            pltpu.VMEM((tb, LANES), F32),
            pltpu.VMEM((tb, LANES), F32),
            pltpu.VMEM((n_chunks, LANES, A_CHUNK), F32),
            pltpu.VMEM((H, tb, DK), F32),
        ],
        compiler_params=_params(2),
        name="delta",
    )(p, p, p, p, ps, cw, arow, brow, onorm)


def _out_proj_kernel(x_ref, o_ref, w_ref, b_ref, h_ref):
    h_ref[...] = (x_ref[...] + b_ref[...]
                  + jnp.dot(o_ref[...], w_ref[...], preferred_element_type=F32))


def _out_proj(x2, o, w, b, *, tm):
    T, D = x2.shape
    K = o.shape[1]
    return pl.pallas_call(
        _out_proj_kernel,
        grid=(T // tm,),
        in_specs=[pl.BlockSpec((tm, D), lambda i: (i, 0)),
                  pl.BlockSpec((tm, K), lambda i: (i, 0)),
                  _const_spec((K, D)), _const_spec((1, D))],
        out_specs=pl.BlockSpec((tm, D), lambda i: (i, 0)),
        out_shape=jax.ShapeDtypeStruct((T, D), F32),
        compiler_params=_params(1),
        name="out_proj",
    )(x2, o, w, b)


def _ffn_kernel(x_ref, g_ref, wu_ref, cw_ref, cb_ref, wd_ref, fg_ref, out_ref,
                halo_scr, up_scr, act_scr, *, tiles_per_seq, n_chunk, final_norm):
    i = pl.program_id(0)
    tm = x_ref.shape[0]
    F = wd_ref.shape[0]

    @pl.when(i % tiles_per_seq == 0)
    def _():
        halo_scr[...] = jnp.zeros_like(halo_scr)

    x = x_ref[...]
    hn = (x * _rms_scale(x) * g_ref[...]).astype(BF16)

    def conv_cols(c0):
        u = jnp.dot(hn, wu_ref[:, c0:c0 + n_chunk], preferred_element_type=F32)
        up_scr[0:SUBLANES, :] = halo_scr[:, c0:c0 + n_chunk]
        up_scr[SUBLANES:SUBLANES + tm, :] = u
        halo_scr[:, c0:c0 + n_chunk] = u[tm - SUBLANES:, :]
        w = cw_ref[:, c0:c0 + n_chunk]
        y = u * w[FFN_CONV - 1:FFN_CONV, :] + cb_ref[:, c0:c0 + n_chunk]
        for j in range(FFN_CONV - 1):
            off = SUBLANES - (FFN_CONV - 1) + j
            y = y + up_scr[off:off + tm, :] * w[j:j + 1, :]
        return y

    for c in range(F // n_chunk):
        gate = conv_cols(c * n_chunk)
        up = conv_cols(F + c * n_chunk)
        act_scr[:, c * n_chunk:(c + 1) * n_chunk] = (_silu(gate) * up).astype(BF16)

    out = x + jnp.dot(act_scr[...], wd_ref[...], preferred_element_type=F32)
    if final_norm:
        out = out * _rms_scale(out) * fg_ref[...]
    out_ref[...] = out


def _ffn(h2, g, w_up, conv_w, conv_b, w_down, final_g, *, seq, tm, final_norm):
    T, D = h2.shape
    F = w_down.shape[0]
    n_chunk = 256
    return pl.pallas_call(
        functools.partial(_ffn_kernel, tiles_per_seq=seq // tm, n_chunk=n_chunk,
                          final_norm=final_norm),
        grid=(T // tm,),
        in_specs=[pl.BlockSpec((tm, D), lambda i: (i, 0)),
                  _const_spec((1, D)),
                  _const_spec((D, 2 * F)),
                  _const_spec((FFN_CONV, 2 * F)),
                  _const_spec((1, 2 * F)),
                  _const_spec((F, D)),
                  _const_spec((1, D))],
        out_specs=pl.BlockSpec((tm, D), lambda i: (i, 0)),
        out_shape=jax.ShapeDtypeStruct((T, D), F32),
        scratch_shapes=[
            pltpu.VMEM((SUBLANES, 2 * F), F32),
            pltpu.VMEM((tm + SUBLANES, n_chunk), F32),
            pltpu.VMEM((tm, F), BF16),
        ],
        compiler_params=_params(1),
        name="ffn",
    )(h2, g, w_up, conv_w, conv_b, w_down, final_g)


def _qkv_proj_kernel(x_ref, gkv_ref, wkv_ref, bkv_ref, gq_ref, wq_ref, bq_ref, kv_ref, q_ref):
    x = x_ref[...]
    xn = x * _rms_scale(x)
    kv = jnp.dot((xn * gkv_ref[...]).astype(BF16), wkv_ref[...], preferred_element_type=F32)
    kv_ref[...] = (kv + bkv_ref[...]).astype(kv_ref.dtype)
    q = jnp.dot((xn * gq_ref[...]).astype(BF16), wq_ref[...], preferred_element_type=F32)
    q_ref[...] = ((q + bq_ref[...]) * (B_DH ** -0.5)).astype(q_ref.dtype)


def _qkv_proj(h2, gkv, wkv, bkv, gq, wq, bq, *, tm):
    T, D = h2.shape
    NKV = wkv.shape[1]
    NQ = wq.shape[1]
    return pl.pallas_call(
        _qkv_proj_kernel,
        grid=(T // tm,),
        in_specs=[pl.BlockSpec((tm, D), lambda i: (i, 0)),
                  _const_spec((1, D)), _const_spec((D, NKV)), _const_spec((1, NKV)),
                  _const_spec((1, D)), _const_spec((D, NQ)), _const_spec((1, NQ))],
        out_specs=[pl.BlockSpec((tm, NKV), lambda i: (i, 0)),
                   pl.BlockSpec((tm, NQ), lambda i: (i, 0))],
        out_shape=[jax.ShapeDtypeStruct((T, NKV), BF16),
                   jax.ShapeDtypeStruct((T, NQ), BF16)],
        compiler_params=_params(1),
        name="qkv_proj",
    )(h2, gkv, wkv, bkv, gq, wq, bq)


def _swa_kernel(sink_ref, q_ref, kvp_ref, kvc_ref, o_ref):
    j = pl.program_id(1)
    W = B_WINDOW
    qi = lax.broadcasted_iota(jnp.int32, (W, 2 * W), 0)
    kj = lax.broadcasted_iota(jnp.int32, (W, 2 * W), 1)
    dist = qi + W - kj
    valid = (dist >= 0) & (dist < W) & ((j > 0) | (kj >= W))
    distf = dist.astype(F32)
    nkv = B_KV_HEADS * B_DH
    for kh in range(B_KV_HEADS):
        lo = kh * B_DH
        kk = jnp.concatenate([kvp_ref[:, lo:lo + B_DH], kvc_ref[:, lo:lo + B_DH]], axis=0)
        vv = jnp.concatenate([kvp_ref[:, nkv + lo:nkv + lo + B_DH],
                              kvc_ref[:, nkv + lo:nkv + lo + B_DH]], axis=0)
        for g in range(B_GROUP):
            head = kh * B_GROUP + g
            slope = 2.0 ** (-8.0 * (head + 1) / B_HEADS)
            qh = q_ref[:, head * B_DH:(head + 1) * B_DH]
            s = lax.dot_general(qh, kk, (((1,), (1,)), ((), ())), preferred_element_type=F32)
            s = jnp.where(valid, s - slope * distf, -jnp.inf)
            sink = sink_ref[head]
            m = jnp.maximum(jnp.max(s, axis=-1, keepdims=True), sink)
            p = jnp.exp(s - m)
            denom = jnp.sum(p, axis=-1, keepdims=True) + jnp.exp(sink - m)
            o = jnp.dot(p.astype(BF16), vv, preferred_element_type=F32) / denom
            o_ref[:, head * B_DH:(head + 1) * B_DH] = o.astype(o_ref.dtype)


def _swa(q, kv, sinks, *, batch, seq):
    T, NQ = q.shape
    NKV2 = kv.shape[1]
    W = B_WINDOW
    nb = seq // W
    return pl.pallas_call(
        _swa_kernel,
        grid=(batch, nb),
        in_specs=[pl.BlockSpec(memory_space=pltpu.SMEM),
                  pl.BlockSpec((W, NQ), lambda b, j: (b * nb + j, 0)),
                  pl.BlockSpec((W, NKV2), lambda b, j: (b * nb + jnp.maximum(j - 1, 0), 0)),
                  pl.BlockSpec((W, NKV2), lambda b, j: (b * nb + j, 0))],
        out_specs=pl.BlockSpec((W, NQ), lambda b, j: (b * nb + j, 0)),
        out_shape=jax.ShapeDtypeStruct((T, NQ), BF16),
        compiler_params=_params(2),
        name="swa",
    )(sinks, q, kv, kv)


def _row(v):
    return v.reshape(1, -1).astype(F32)


def _delta_layer(h2, norm_g, w_in, conv_w, a_log, dt_bias, onorm, w_out, *, batch, seq):
    D = h2.shape[1]
    H = A_HEADS
    n_main = 4 * H * A_DK
    w_main = w_in[:, :n_main].astype(BF16)
    w_small = jnp.pad(w_in[:, n_main:], ((0, 0), (0, LANES - 2 * H))).astype(BF16)
    p, ps = _in_proj(h2, _row(norm_g), w_main, w_small, tm=256)
    cw = conv_w.reshape(A_CONV, 3, H, A_DK).transpose(1, 2, 0, 3)
    arow = jnp.pad(a_log, (H, LANES - 2 * H)).reshape(1, LANES)
    brow = jnp.pad(dt_bias, (H, LANES - 2 * H)).reshape(1, LANES)
    o = _delta(p, ps, cw, arow, brow, _row(onorm), batch=batch, seq=seq, tb=256)
    return _out_proj(h2, o, w_out.astype(BF16), jnp.zeros((1, D), F32), tm=512)


def _ffn_layer(h2, norm_g, w_up, conv_w, conv_b, w_down, final_g, *, seq, final_norm):
    return _ffn(h2, _row(norm_g), w_up.astype(BF16), conv_w, _row(conv_b), w_down.astype(BF16),
                _row(final_g), seq=seq, tm=512, final_norm=final_norm)


def kernel(x, a_norm, a_w_in, a_conv_w, a_A_log, a_dt_bias, a_onorm, a_w_out, kv_norm, kv_w, kv_b, b_norm, b_w_q, b_b_q, b_sinks, b_w_o, b_b_o, f_norm, f_w_up, f_conv_w, f_conv_b, f_w_down, final_norm):
    B, S, D = x.shape
    depth = f_norm.shape[0]
    n_a = a_norm.shape[0]
    h = x.reshape(B * S, D)
    kv = None
    for layer in range(depth):
        if layer < n_a:
            i = layer
            h = _delta_layer(h, a_norm[i], a_w_in[i], a_conv_w[i], a_A_log[i], a_dt_bias[i],
                             a_onorm[i], a_w_out[i], batch=B, seq=S)
            q = None
        else:
            i = layer - n_a
            gkv = kv_norm if i == 0 else b_norm[i]
            kv_new, q = _qkv_proj(h, _row(gkv), kv_w.astype(BF16), _row(kv_b),
                                  _row(b_norm[i]), b_w_q[i].astype(BF16), _row(b_b_q[i]), tm=512)
            if i == 0:
                kv = kv_new
            o = _swa(q, kv, b_sinks[i].astype(F32), batch=B, seq=S)
            h = _out_proj(h, o, b_w_o[i].astype(BF16), _row(b_b_o[i]), tm=512)
        h = _ffn_layer(h, f_norm[layer], f_w_up[layer], f_conv_w[layer], f_conv_b[layer],
                       f_w_down[layer], final_norm, seq=S, final_norm=(layer == depth - 1))
    return h.reshape(B, S, D)
```

```python
import functools

import jax
import jax.numpy as jnp
from jax import lax
from jax.experimental import pallas as pl
from jax.experimental.pallas import tpu as pltpu

F32 = jnp.float32
BF16 = jnp.bfloat16
EPS = 1e-6

LANES = 128
SUBLANES = 8
VMEM_LIMIT = 56 * 1024 * 1024

A_HEADS = 8
A_DK = 128
A_CONV = 4
A_CHUNK = 64

B_HEADS = 16
B_KV_HEADS = 4
B_GROUP = B_HEADS // B_KV_HEADS
B_DH = 64
B_WINDOW = 128

FFN_CONV = 3


def _params(n_axes, vmem=VMEM_LIMIT):
    return pltpu.CompilerParams(dimension_semantics=("arbitrary",) * n_axes,
                                vmem_limit_bytes=vmem)


def _const_spec(shape):
    nd = len(shape)
    return pl.BlockSpec(shape, lambda *_: (0,) * nd)


def _rms_scale(x):
    return lax.rsqrt(jnp.mean(x * x, axis=-1, keepdims=True) + EPS)


def _sigmoid(x):
    return 1.0 / (1.0 + jnp.exp(-x))


def _silu(x):
    return x * _sigmoid(x)


def _softplus(x):
    return jnp.maximum(x, 0.0) + jnp.log(1.0 + jnp.exp(-jnp.abs(x)))


def _in_proj_kernel(x_ref, g_ref, w_ref, ws_ref, p_ref, ps_ref, *, n_chunk):
    x = x_ref[...]
    hn = (x * _rms_scale(x) * g_ref[...]).astype(BF16)
    n_slabs = p_ref.shape[0]
    per = n_chunk // LANES
    for c in range(n_slabs // per):
        res = jnp.dot(hn, w_ref[:, c * n_chunk:(c + 1) * n_chunk], preferred_element_type=F32)
        for j in range(per):
            p_ref[c * per + j] = res[:, j * LANES:(j + 1) * LANES]
    ps_ref[...] = jnp.dot(hn, ws_ref[...], preferred_element_type=F32)


def _in_proj(x2, g, w_main, w_small, *, tm):
    T, D = x2.shape
    N = w_main.shape[1]
    n_slabs = N // LANES
    return pl.pallas_call(
        functools.partial(_in_proj_kernel, n_chunk=512),
        grid=(T // tm,),
        in_specs=[pl.BlockSpec((tm, D), lambda i: (i, 0)),
                  _const_spec((1, D)),
                  _const_spec((D, N)),
                  _const_spec((D, LANES))],
        out_specs=[pl.BlockSpec((n_slabs, tm, LANES), lambda i: (0, i, 0)),
                   pl.BlockSpec((tm, LANES), lambda i: (i, 0))],
        out_shape=[jax.ShapeDtypeStruct((n_slabs, T, LANES), F32),
                   jax.ShapeDtypeStruct((T, LANES), F32)],
        compiler_params=_params(1),
        name="in_proj",
    )(x2, g, w_main, w_small)


def _unit_lower_inverses(lmats):
    n = lmats[0].shape[0]
    n_tiles = n // SUBLANES
    lane = lax.broadcasted_iota(jnp.int32, (SUBLANES, n), 1)
    sub = lax.broadcasted_iota(jnp.int32, (SUBLANES, n), 0)
    blk = lane // SUBLANES
    base = blk * SUBLANES
    eye = jnp.where(lane - base == sub, 1.0, 0.0).astype(F32)
    zero = jnp.zeros((SUBLANES, n), F32)
    lts = [[lm[t * SUBLANES:(t + 1) * SUBLANES, :] for t in range(n_tiles)] for lm in lmats]
    tts = []
    for lt in lts:
        ld = lt[n_tiles - 1]
        for t in reversed(range(n_tiles - 1)):
            ld = jnp.where(blk == t, lt[t], ld)
        td = eye
        for m in range(SUBLANES - 1):
            col = jnp.take_along_axis(ld, base + m, axis=1)
            td = td - col * td[m:m + 1, :]
        tts.append([jnp.where(blk == t, td, 0.0) for t in range(n_tiles)])
    s = SUBLANES
    while s < n:
        odd = [t for t in range(n_tiles) if (t * SUBLANES // s) % 2 == 1]
        ws = []
        for lt, tt in zip(lts, tts):
            l_odd = jnp.concatenate(
                [jnp.where(lane // s == t * SUBLANES // s - 1, lt[t], 0.0) for t in odd], axis=0)
            t_full = jnp.concatenate(tt, axis=0).astype(BF16)
            ws.append(jnp.dot(l_odd.astype(BF16), t_full, preferred_element_type=F32))
        xs = []
        for tt, w in zip(tts, ws):
            w_rows = {t: w[i * SUBLANES:(i + 1) * SUBLANES, :] for i, t in enumerate(odd)}
            w_full = jnp.concatenate([w_rows.get(t, zero) for t in range(n_tiles)], axis=0)
            t_odd = jnp.concatenate([tt[t] for t in odd], axis=0).astype(BF16)
            xs.append(jnp.dot(t_odd, w_full.astype(BF16), preferred_element_type=F32))
        for tt, x in zip(tts, xs):
            for i, t in enumerate(odd):
                tt[t] = tt[t] - x[i * SUBLANES:(i + 1) * SUBLANES, :]
        s *= 2
    return [jnp.concatenate(tt, axis=0).astype(BF16) for tt in tts]


def _delta_kernel(q_ref, k_ref, v_ref, gt_ref, ps_ref, cw_ref, arow_ref, brow_ref, on_ref,
                  o_ref,
                  state_scr, halo_scr, xp_scr, kv_scr, kq_scr, kt_scr, gc_scr, beta_scr, grow_scr,
                  attn_scr, val_scr, egc_scr, dl_scr, gam_scr, o_scr, *, group):
    s = pl.program_id(1)
    H, Tb, DK = q_ref.shape
    C = A_CHUNK
    n_chunks = Tb // C

    @pl.when(s == 0)
    def _():
        state_scr[...] = jnp.zeros_like(state_scr)
        halo_scr[...] = jnp.zeros_like(halo_scr)

    for sec, ref in enumerate((q_ref, k_ref, v_ref)):
        for h in range(H):
            x = ref[h]
            xp_scr[0:SUBLANES, :] = halo_scr[sec, h]
            xp_scr[SUBLANES:SUBLANES + Tb, :] = x
            halo_scr[sec, h] = x[Tb - SUBLANES:, :]
            w = cw_ref[sec, h]
            y = x * w[A_CONV - 1:A_CONV, :]
            for j in range(A_CONV - 1):
                off = SUBLANES - (A_CONV - 1) + j
                y = y + xp_scr[off:off + Tb, :] * w[j:j + 1, :]
            y = _silu(y)
            if sec < 2:
                y = y * lax.rsqrt(jnp.sum(y * y, axis=-1, keepdims=True) + EPS)
            if sec == 0:
                yb = (y * (DK ** -0.5)).astype(BF16)
                for c in range(n_chunks):
                    kq_scr[h, c, C:2 * C, :] = yb[c * C:(c + 1) * C, :]
            else:
                kv_scr[sec - 1, h] = y
            if sec == 1:
                yb = y.astype(BF16)
                ytb = y.T.astype(BF16)
                for c in range(n_chunks):
                    kq_scr[h, c, 0:C, :] = yb[c * C:(c + 1) * C, :]
                    kt_scr[h, c] = ytb[:, c * C:(c + 1) * C]

    ps = ps_ref[...]
    beta_scr[...] = _sigmoid(ps)
    g = -jnp.exp(arow_ref[...]) * _softplus(ps + brow_ref[...])
    pos = lax.broadcasted_iota(jnp.int32, (Tb, LANES), 0) % C
    gc = g
    sh = 1
    while sh < C:
        gc = gc + jnp.where(pos >= sh, pltpu.roll(gc, sh, 0), 0.0)
        sh *= 2
    gc_scr[...] = gc
    gct = gc.T
    for c in range(n_chunks):
        grow_scr[c] = gct[:, c * C:(c + 1) * C]

    ri = lax.broadcasted_iota(jnp.int32, (C, C), 0)
    ci = lax.broadcasted_iota(jnp.int32, (C, C), 1)
    incl = ri >= ci
    strict = ri > ci

    def gate_cols(c, h):
        gcol = jnp.broadcast_to(gc_scr[c * C:(c + 1) * C, H + h:H + h + 1], (C, DK))
        bcol = jnp.broadcast_to(beta_scr[c * C:(c + 1) * C, h:h + 1], (C, DK))
        return gcol, bcol

    for c0 in range(0, n_chunks, group):
        chains = [(c, h) for c in range(c0, c0 + group) for h in range(H)]
        scs = [jnp.dot(kq_scr[h, c], kt_scr[h, c], preferred_element_type=F32)
               for c, h in chains]
        lmats = []
        for (c, h), sc in zip(chains, scs):
            gcol, bcol = gate_cols(c, h)
            grow = grow_scr[c, H + h:H + h + 1, :]
            decay = jnp.exp(jnp.where(incl, gcol[:, :C] - grow, -jnp.inf))
            lmats.append(jnp.where(strict, sc[:C] * (decay * bcol[:, :C]), 0.0))
            attn_scr[h, c] = (sc[C:] * decay).astype(BF16)
        tmats = _unit_lower_inverses(lmats)
        sols = []
        for (c, h), tmat in zip(chains, tmats):
            gcol, bcol = gate_cols(c, h)
            glast = gcol[C - 1:C, :]
            egc = jnp.exp(gcol)
            k = kv_scr[0, h, c * C:(c + 1) * C, :]
            v = kv_scr[1, h, c * C:(c + 1) * C, :]
            rhs = jnp.concatenate([k * (bcol * egc), v * bcol], axis=1).astype(BF16)
            sols.append(jnp.dot(tmat, rhs, preferred_element_type=F32))
            egc_scr[h, c] = egc
            dl_scr[h, c] = jnp.exp(glast - gcol)
            gam_scr[h, c] = jnp.broadcast_to(jnp.exp(glast), (SUBLANES, DK))
        for (c, h), sol in zip(chains, sols):
            kq_scr[h, c, 0:C, :] = sol[:, :DK].astype(BF16)
            val_scr[h, c] = sol[:, DK:]

    for c in range(n_chunks):
        sts = [state_scr[h] for h in range(H)]
        prs = [jnp.dot(kq_scr[h, c], sts[h].astype(BF16), preferred_element_type=F32)
               for h in range(H)]
        v_news = [val_scr[h, c] - prs[h][:C] for h in range(H)]
        ats = [jnp.dot(attn_scr[h, c], v_news[h].astype(BF16), preferred_element_type=F32)
               for h in range(H)]
        kvs = [jnp.dot(kt_scr[h, c], (dl_scr[h, c] * v_news[h]).astype(BF16),
                       preferred_element_type=F32) for h in range(H)]
        for h in range(H):
            state_scr[h] = sts[h] * gam_scr[h, c][0:1, :] + kvs[h]
            o_scr[h, c * C:(c + 1) * C, :] = egc_scr[h, c] * prs[h][C:] + ats[h]

    for h in range(H):
        o = o_scr[h]
        gate = gt_ref[h]
        y = o * _rms_scale(o) * on_ref[...] * _silu(gate)
        o_ref[:, h * DK:(h + 1) * DK] = y.astype(o_ref.dtype)


def _delta(p, ps, cw, arow, brow, onorm, *, batch, seq, tb):
    H = A_HEADS
    DK = A_DK
    T = batch * seq
    ns = seq // tb
    n_chunks = tb // A_CHUNK

    def sec_spec(sec):
        return pl.BlockSpec((H, tb, DK), lambda b, s: (sec, b * ns + s, 0))

    return pl.pallas_call(
        functools.partial(_delta_kernel, group=2),
        grid=(batch, ns),
        in_specs=[sec_spec(0), sec_spec(1), sec_spec(2), sec_spec(3),
                  pl.BlockSpec((tb, LANES), lambda b, s: (b * ns + s, 0)),
                  _const_spec((3, H, A_CONV, DK)),
                  _const_spec((1, LANES)), _const_spec((1, LANES)), _const_spec((1, DK))],
        out_specs=pl.BlockSpec((tb, H * DK), lambda b, s: (b * ns + s, 0)),
        out_shape=jax.ShapeDtypeStruct((T, H * DK), BF16),
        scratch_shapes=[
            pltpu.VMEM((H, DK, DK), F32),
            pltpu.VMEM((3, H, SUBLANES, DK), F32),
            pltpu.VMEM((tb + SUBLANES, DK), F32),
            pltpu.VMEM((2, H, tb, DK), F32),
            pltpu.VMEM((H, n_chunks, 2 * A_CHUNK, DK), BF16),
            pltpu.VMEM((H, n_chunks, DK, A_CHUNK), BF16),
            pltpu.VMEM((tb, LANES), F32),
            pltpu.VMEM((tb, LANES), F32),
            pltpu.VMEM((n_chunks, LANES, A_CHUNK), F32),
            pltpu.VMEM((H, n_chunks, A_CHUNK, A_CHUNK), BF16),
            pltpu.VMEM((H, n_chunks, A_CHUNK, DK), F32),
            pltpu.VMEM((H, n_chunks, A_CHUNK, DK), F32),
            pltpu.VMEM((H, n_chunks, A_CHUNK, DK), F32),
            pltpu.VMEM((H, n_chunks, SUBLANES, DK), F32),
            pltpu.VMEM((H, tb, DK), F32),
        ],
        compiler_params=_params(2),
        name="delta",
    )(p, p, p, p, ps, cw, arow, brow, onorm)


def _out_proj_kernel(x_ref, o_ref, w_ref, b_ref, h_ref):
    h_ref[...] = (x_ref[...] + b_ref[...]
                  + jnp.dot(o_ref[...], w_ref[...], preferred_element_type=F32))


def _out_proj(x2, o, w, b, *, tm):
    T, D = x2.shape
    K = o.shape[1]
    return pl.pallas_call(
        _out_proj_kernel,
        grid=(T // tm,),
        in_specs=[pl.BlockSpec((tm, D), lambda i: (i, 0)),
                  pl.BlockSpec((tm, K), lambda i: (i, 0)),
                  _const_spec((K, D)), _const_spec((1, D))],
        out_specs=pl.BlockSpec((tm, D), lambda i: (i, 0)),
        out_shape=jax.ShapeDtypeStruct((T, D), F32),
        compiler_params=_params(1),
        name="out_proj",
    )(x2, o, w, b)


def _ffn_kernel(x_ref, g_ref, wu_ref, cw_ref, cb_ref, wd_ref, fg_ref, out_ref,
                halo_scr, up_scr, act_scr, *, tiles_per_seq, n_chunk, final_norm):
    i = pl.program_id(0)
    tm = x_ref.shape[0]
    F = wd_ref.shape[0]

    @pl.when(i % tiles_per_seq == 0)
    def _():
        halo_scr[...] = jnp.zeros_like(halo_scr)

    x = x_ref[...]
    hn = (x * _rms_scale(x) * g_ref[...]).astype(BF16)

    def conv_cols(c0):
        u = jnp.dot(hn, wu_ref[:, c0:c0 + n_chunk], preferred_element_type=F32)
        up_scr[0:SUBLANES, :] = halo_scr[:, c0:c0 + n_chunk]
        up_scr[SUBLANES:SUBLANES + tm, :] = u
        halo_scr[:, c0:c0 + n_chunk] = u[tm - SUBLANES:, :]
        w = cw_ref[:, c0:c0 + n_chunk]
        y = u * w[FFN_CONV - 1:FFN_CONV, :] + cb_ref[:, c0:c0 + n_chunk]
        for j in range(FFN_CONV - 1):
            off = SUBLANES - (FFN_CONV - 1) + j
            y = y + up_scr[off:off + tm, :] * w[j:j + 1, :]
        return y

    for c in range(F // n_chunk):
        gate = conv_cols(c * n_chunk)
        up = conv_cols(F + c * n_chunk)
        act_scr[:, c * n_chunk:(c + 1) * n_chunk] = (_silu(gate) * up).astype(BF16)

    out = x + jnp.dot(act_scr[...], wd_ref[...], preferred_element_type=F32)
    if final_norm:
        out = out * _rms_scale(out) * fg_ref[...]
    out_ref[...] = out


def _ffn(h2, g, w_up, conv_w, conv_b, w_down, final_g, *, seq, tm, final_norm):
    T, D = h2.shape
    F = w_down.shape[0]
    n_chunk = 256
    return pl.pallas_call(
        functools.partial(_ffn_kernel, tiles_per_seq=seq // tm, n_chunk=n_chunk,
                          final_norm=final_norm),
        grid=(T // tm,),
        in_specs=[pl.BlockSpec((tm, D), lambda i: (i, 0)),
                  _const_spec((1, D)),
                  _const_spec((D, 2 * F)),
                  _const_spec((FFN_CONV, 2 * F)),
                  _const_spec((1, 2 * F)),
                  _const_spec((F, D)),
                  _const_spec((1, D))],
        out_specs=pl.BlockSpec((tm, D), lambda i: (i, 0)),
        out_shape=jax.ShapeDtypeStruct((T, D), F32),
        scratch_shapes=[
            pltpu.VMEM((SUBLANES, 2 * F), F32),
            pltpu.VMEM((tm + SUBLANES, n_chunk), F32),
            pltpu.VMEM((tm, F), BF16),
        ],
        compiler_params=_params(1),
        name="ffn",
    )(h2, g, w_up, conv_w, conv_b, w_down, final_g)


def _qkv_proj_kernel(x_ref, gkv_ref, wkv_ref, bkv_ref, gq_ref, wq_ref, bq_ref, kv_ref, q_ref):
    x = x_ref[...]
    xn = x * _rms_scale(x)
    kv = jnp.dot((xn * gkv_ref[...]).astype(BF16), wkv_ref[...], preferred_element_type=F32)
    kv_ref[...] = (kv + bkv_ref[...]).astype(kv_ref.dtype)
    q = jnp.dot((xn * gq_ref[...]).astype(BF16), wq_ref[...], preferred_element_type=F32)
    q_ref[...] = ((q + bq_ref[...]) * (B_DH ** -0.5)).astype(q_ref.dtype)


def _qkv_proj(h2, gkv, wkv, bkv, gq, wq, bq, *, tm):
    T, D = h2.shape
    NKV = wkv.shape[1]
    NQ = wq.shape[1]
    return pl.pallas_call(
        _qkv_proj_kernel,
        grid=(T // tm,),
        in_specs=[pl.BlockSpec((tm, D), lambda i: (i, 0)),
                  _const_spec((1, D)), _const_spec((D, NKV)), _const_spec((1, NKV)),
                  _const_spec((1, D)), _const_spec((D, NQ)), _const_spec((1, NQ))],
        out_specs=[pl.BlockSpec((tm, NKV), lambda i: (i, 0)),
                   pl.BlockSpec((tm, NQ), lambda i: (i, 0))],
        out_shape=[jax.ShapeDtypeStruct((T, NKV), BF16),
                   jax.ShapeDtypeStruct((T, NQ), BF16)],
        compiler_params=_params(1),
        name="qkv_proj",
    )(h2, gkv, wkv, bkv, gq, wq, bq)


def _swa_kernel(sink_ref, q_ref, kvp_ref, kvc_ref, o_ref):
    j = pl.program_id(1)
    W = B_WINDOW
    qi = lax.broadcasted_iota(jnp.int32, (W, 2 * W), 0)
    kj = lax.broadcasted_iota(jnp.int32, (W, 2 * W), 1)
    dist = qi + W - kj
    valid = (dist >= 0) & (dist < W) & ((j > 0) | (kj >= W))
    distf = dist.astype(F32)
    nkv = B_KV_HEADS * B_DH
    for kh in range(B_KV_HEADS):
        lo = kh * B_DH
        kk = jnp.concatenate([kvp_ref[:, lo:lo + B_DH], kvc_ref[:, lo:lo + B_DH]], axis=0)
        vv = jnp.concatenate([kvp_ref[:, nkv + lo:nkv + lo + B_DH],
                              kvc_ref[:, nkv + lo:nkv + lo + B_DH]], axis=0)
        for g in range(B_GROUP):
            head = kh * B_GROUP + g
            slope = 2.0 ** (-8.0 * (head + 1) / B_HEADS)
            qh = q_ref[:, head * B_DH:(head + 1) * B_DH]
            s = lax.dot_general(qh, kk, (((1,), (1,)), ((), ())), preferred_element_type=F32)
            s = jnp.where(valid, s - slope * distf, -jnp.inf)
            sink = sink_ref[head]
            m = jnp.maximum(jnp.max(s, axis=-1, keepdims=True), sink)
            p = jnp.exp(s - m)
            denom = jnp.sum(p, axis=-1, keepdims=True) + jnp.exp(sink - m)
            o = jnp.dot(p.astype(BF16), vv, preferred_element_type=F32) / denom
            o_ref[:, head * B_DH:(head + 1) * B_DH] = o.astype(o_ref.dtype)


def _swa(q, kv, sinks, *, batch, seq):
    T, NQ = q.shape
    NKV2 = kv.shape[1]
    W = B_WINDOW
    nb = seq // W
    return pl.pallas_call(
        _swa_kernel,
        grid=(batch, nb),
        in_specs=[pl.BlockSpec(memory_space=pltpu.SMEM),
                  pl.BlockSpec((W, NQ), lambda b, j: (b * nb + j, 0)),
                  pl.BlockSpec((W, NKV2), lambda b, j: (b * nb + jnp.maximum(j - 1, 0), 0)),
                  pl.BlockSpec((W, NKV2), lambda b, j: (b * nb + j, 0))],
        out_specs=pl.BlockSpec((W, NQ), lambda b, j: (b * nb + j, 0)),
        out_shape=jax.ShapeDtypeStruct((T, NQ), BF16),
        compiler_params=_params(2),
        name="swa",
    )(sinks, q, kv, kv)


def _row(v):
    return v.reshape(1, -1).astype(F32)


def _delta_layer(h2, norm_g, w_in, conv_w, a_log, dt_bias, onorm, w_out, *, batch, seq):
    D = h2.shape[1]
    H = A_HEADS
    n_main = 4 * H * A_DK
    w_main = w_in[:, :n_main].astype(BF16)
    w_small = jnp.pad(w_in[:, n_main:], ((0, 0), (0, LANES - 2 * H))).astype(BF16)
    p, ps = _in_proj(h2, _row(norm_g), w_main, w_small, tm=256)
    cw = conv_w.reshape(A_CONV, 3, H, A_DK).transpose(1, 2, 0, 3)
    arow = jnp.pad(a_log, (H, LANES - 2 * H)).reshape(1, LANES)
    brow = jnp.pad(dt_bias, (H, LANES - 2 * H)).reshape(1, LANES)
    o = _delta(p, ps, cw, arow, brow, _row(onorm), batch=batch, seq=seq, tb=256)
    return _out_proj(h2, o, w_out.astype(BF16), jnp.zeros((1, D), F32), tm=512)


def _ffn_layer(h2, norm_g, w_up, conv_w, conv_b, w_down, final_g, *, seq, final_norm):
    return _ffn(h2, _row(norm_g), w_up.astype(BF16), conv_w, _row(conv_b), w_down.astype(BF16),
                _row(final_g), seq=seq, tm=512, final_norm=final_norm)


def kernel(x, a_norm, a_w_in, a_conv_w, a_A_log, a_dt_bias, a_onorm, a_w_out, kv_norm, kv_w, kv_b, b_norm, b_w_q, b_b_q, b_sinks, b_w_o, b_b_o, f_norm, f_w_up, f_conv_w, f_conv_b, f_w_down, final_norm):
    B, S, D = x.shape
    depth = f_norm.shape[0]
    n_a = a_norm.shape[0]
    h = x.reshape(B * S, D)
    kv = None
    for layer in range(depth):
        if layer < n_a:
            i = layer
            h = _delta_layer(h, a_norm[i], a_w_in[i], a_conv_w[i], a_A_log[i], a_dt_bias[i],
                             a_onorm[i], a_w_out[i], batch=B, seq=S)
            q = None
        else:
            i = layer - n_a
            gkv = kv_norm if i == 0 else b_norm[i]
            kv_new, q = _qkv_proj(h, _row(gkv), kv_w.astype(BF16), _row(kv_b),
                                  _row(b_norm[i]), b_w_q[i].astype(BF16), _row(b_b_q[i]), tm=512)
            if i == 0:
                kv = kv_new
            o = _swa(q, kv, b_sinks[i].astype(F32), batch=B, seq=S)
            h = _out_proj(h, o, b_w_o[i].astype(BF16), _row(b_b_o[i]), tm=512)
        h = _ffn_layer(h, f_norm[layer], f_w_up[layer], f_conv_w[layer], f_conv_b[layer],
                       f_w_down[layer], final_norm, seq=S, final_norm=(layer == depth - 1))
    return h.reshape(B, S, D)
```

```python
import functools

import jax
import jax.numpy as jnp
from jax import lax
from jax.experimental import pallas as pl
from jax.experimental.pallas import tpu as pltpu

F32 = jnp.float32
BF16 = jnp.bfloat16
EPS = 1e-6

LANES = 128
SUBLANES = 8
VMEM_LIMIT = 56 * 1024 * 1024

A_HEADS = 8
A_DK = 128
A_CONV = 4
A_CHUNK = 64

B_HEADS = 16
B_KV_HEADS = 4
B_GROUP = B_HEADS // B_KV_HEADS
B_DH = 64
B_WINDOW = 128

FFN_CONV = 3


def _params(n_axes, vmem=VMEM_LIMIT):
    return pltpu.CompilerParams(dimension_semantics=("arbitrary",) * n_axes,
                                vmem_limit_bytes=vmem)


def _const_spec(shape):
    nd = len(shape)
    return pl.BlockSpec(shape, lambda *_: (0,) * nd)


def _rms_scale(x):
    return lax.rsqrt(jnp.mean(x * x, axis=-1, keepdims=True) + EPS)


def _sigmoid(x):
    return 1.0 / (1.0 + jnp.exp(-x))


def _silu(x):
    return x * _sigmoid(x)


def _softplus(x):
    return jnp.maximum(x, 0.0) + jnp.log(1.0 + jnp.exp(-jnp.abs(x)))


def _in_proj_kernel(x_ref, g_ref, w_ref, ws_ref, p_ref, ps_ref, *, n_chunk):
    x = x_ref[...]
    hn = (x * _rms_scale(x) * g_ref[...]).astype(BF16)
    n_slabs = p_ref.shape[0]
    per = n_chunk // LANES
    for c in range(n_slabs // per):
        res = jnp.dot(hn, w_ref[:, c * n_chunk:(c + 1) * n_chunk], preferred_element_type=F32)
        for j in range(per):
            p_ref[c * per + j] = res[:, j * LANES:(j + 1) * LANES]
    ps_ref[...] = jnp.dot(hn, ws_ref[...], preferred_element_type=F32)


def _in_proj(x2, g, w_main, w_small, *, tm):
    T, D = x2.shape
    N = w_main.shape[1]
    n_slabs = N // LANES
    return pl.pallas_call(
        functools.partial(_in_proj_kernel, n_chunk=512),
        grid=(T // tm,),
        in_specs=[pl.BlockSpec((tm, D), lambda i: (i, 0)),
                  _const_spec((1, D)),
                  _const_spec((D, N)),
                  _const_spec((D, LANES))],
        out_specs=[pl.BlockSpec((n_slabs, tm, LANES), lambda i: (0, i, 0)),
                   pl.BlockSpec((tm, LANES), lambda i: (i, 0))],
        out_shape=[jax.ShapeDtypeStruct((n_slabs, T, LANES), F32),
                   jax.ShapeDtypeStruct((T, LANES), F32)],
        compiler_params=_params(1),
        name="in_proj",
    )(x2, g, w_main, w_small)


def _unit_lower_inverses(lmats):
    n = lmats[0].shape[0]
    n_tiles = n // SUBLANES
    lane = lax.broadcasted_iota(jnp.int32, (SUBLANES, n), 1)
    sub = lax.broadcasted_iota(jnp.int32, (SUBLANES, n), 0)
    blk = lane // SUBLANES
    base = blk * SUBLANES
    eye = jnp.where(lane - base == sub, 1.0, 0.0).astype(F32)
    zero = jnp.zeros((SUBLANES, n), F32)
    lts = [[lm[t * SUBLANES:(t + 1) * SUBLANES, :] for t in range(n_tiles)] for lm in lmats]
    tts = []
    for lt in lts:
        ld = lt[n_tiles - 1]
        for t in reversed(range(n_tiles - 1)):
            ld = jnp.where(blk == t, lt[t], ld)
        td = eye
        for m in range(SUBLANES - 1):
            col = jnp.take_along_axis(ld, base + m, axis=1)
            td = td - col * td[m:m + 1, :]
        tts.append([jnp.where(blk == t, td, 0.0) for t in range(n_tiles)])
    s = SUBLANES
    while s < n:
        odd = [t for t in range(n_tiles) if (t * SUBLANES // s) % 2 == 1]
        ws = []
        for lt, tt in zip(lts, tts):
            l_odd = jnp.concatenate(
                [jnp.where(lane // s == t * SUBLANES // s - 1, lt[t], 0.0) for t in odd], axis=0)
            t_full = jnp.concatenate(tt, axis=0).astype(BF16)
            ws.append(jnp.dot(l_odd.astype(BF16), t_full, preferred_element_type=F32))
        xs = []
        for tt, w in zip(tts, ws):
            w_rows = {t: w[i * SUBLANES:(i + 1) * SUBLANES, :] for i, t in enumerate(odd)}
            w_full = jnp.concatenate([w_rows.get(t, zero) for t in range(n_tiles)], axis=0)
            t_odd = jnp.concatenate([tt[t] for t in odd], axis=0).astype(BF16)
            xs.append(jnp.dot(t_odd, w_full.astype(BF16), preferred_element_type=F32))
        for tt, x in zip(tts, xs):
            for i, t in enumerate(odd):
                tt[t] = tt[t] - x[i * SUBLANES:(i + 1) * SUBLANES, :]
        s *= 2
    return [jnp.concatenate(tt, axis=0).astype(BF16) for tt in tts]


def _delta_kernel(q_ref, k_ref, v_ref, gt_ref, ps_ref, cw_ref, arow_ref, brow_ref, on_ref,
                  o_ref,
                  state_scr, halo_scr, xp_scr, kv_scr, kq_scr, kt_scr, gc_scr, beta_scr, grow_scr,
                  attn_scr, val_scr, egc_scr, dl_scr, gam_scr, o_scr, *, group):
    s = pl.program_id(1)
    H, Tb, DK = q_ref.shape
    C = A_CHUNK
    n_chunks = Tb // C

    @pl.when(s == 0)
    def _():
        state_scr[...] = jnp.zeros_like(state_scr)
        halo_scr[...] = jnp.zeros_like(halo_scr)

    for sec, ref in enumerate((q_ref, k_ref, v_ref)):
        for h in range(H):
            x = ref[h]
            xp_scr[0:SUBLANES, :] = halo_scr[sec, h]
            xp_scr[SUBLANES:SUBLANES + Tb, :] = x
            halo_scr[sec, h] = x[Tb - SUBLANES:, :]
            w = cw_ref[sec, h]
            y = x * w[A_CONV - 1:A_CONV, :]
            for j in range(A_CONV - 1):
                off = SUBLANES - (A_CONV - 1) + j
                y = y + xp_scr[off:off + Tb, :] * w[j:j + 1, :]
            y = _silu(y)
            if sec < 2:
                y = y * lax.rsqrt(jnp.sum(y * y, axis=-1, keepdims=True) + EPS)
            if sec == 0:
                yb = (y * (DK ** -0.5)).astype(BF16)
                for c in range(n_chunks):
                    kq_scr[h, c, C:2 * C, :] = yb[c * C:(c + 1) * C, :]
            else:
                kv_scr[sec - 1, h] = y
            if sec == 1:
                yb = y.astype(BF16)
                ytb = y.T.astype(BF16)
                for c in range(n_chunks):
                    kq_scr[h, c, 0:C, :] = yb[c * C:(c + 1) * C, :]
                    kt_scr[h, c] = ytb[:, c * C:(c + 1) * C]

    ps = ps_ref[...]
    beta_scr[...] = _sigmoid(ps)
    g = -jnp.exp(arow_ref[...]) * _softplus(ps + brow_ref[...])
    pos = lax.broadcasted_iota(jnp.int32, (Tb, LANES), 0) % C
    gc = g
    sh = 1
    while sh < C:
        gc = gc + jnp.where(pos >= sh, pltpu.roll(gc, sh, 0), 0.0)
        sh *= 2
    gc_scr[...] = gc
    gct = gc.T
    for c in range(n_chunks):
        grow_scr[c] = gct[:, c * C:(c + 1) * C]

    ri = lax.broadcasted_iota(jnp.int32, (C, C), 0)
    ci = lax.broadcasted_iota(jnp.int32, (C, C), 1)
    incl = ri >= ci
    strict = ri > ci

    def gate_cols(c, h):
        gcol = jnp.broadcast_to(gc_scr[c * C:(c + 1) * C, H + h:H + h + 1], (C, DK))
        bcol = jnp.broadcast_to(beta_scr[c * C:(c + 1) * C, h:h + 1], (C, DK))
        return gcol, bcol

    for c0 in range(0, n_chunks, group):
        chains = [(c, h) for c in range(c0, c0 + group) for h in range(H)]
        scs = [jnp.dot(kq_scr[h, c], kt_scr[h, c], preferred_element_type=F32)
               for c, h in chains]
        lmats = []
        for (c, h), sc in zip(chains, scs):
            gcol, bcol = gate_cols(c, h)
            grow = grow_scr[c, H + h:H + h + 1, :]
            decay = jnp.exp(jnp.where(incl, gcol[:, :C] - grow, -jnp.inf))
            lmats.append(jnp.where(strict, sc[:C] * (decay * bcol[:, :C]), 0.0))
            attn_scr[h, c] = (sc[C:] * decay).astype(BF16)
        tmats = _unit_lower_inverses(lmats)
        sols = []
        for (c, h), tmat in zip(chains, tmats):
            gcol, bcol = gate_cols(c, h)
            glast = gcol[C - 1:C, :]
            egc = jnp.exp(gcol)
            k = kv_scr[0, h, c * C:(c + 1) * C, :]
            v = kv_scr[1, h, c * C:(c + 1) * C, :]
            rhs = jnp.concatenate([k * (bcol * egc), v * bcol], axis=1).astype(BF16)
            sols.append(jnp.dot(tmat, rhs, preferred_element_type=F32))
            egc_scr[h, c] = egc
            dl_scr[h, c] = jnp.exp(glast - gcol)
            gam_scr[h, c] = jnp.broadcast_to(jnp.exp(glast), (SUBLANES, DK))
        for (c, h), sol in zip(chains, sols):
            kq_scr[h, c, 0:C, :] = sol[:, :DK].astype(BF16)
            val_scr[h, c] = sol[:, DK:]

    for c in range(n_chunks):
        sts = [state_scr[h] for h in range(H)]
        prs = [jnp.dot(kq_scr[h, c], sts[h].astype(BF16), preferred_element_type=F32)
               for h in range(H)]
        v_news = [val_scr[h, c] - prs[h][:C] for h in range(H)]
        ats = [jnp.dot(attn_scr[h, c], v_news[h].astype(BF16), preferred_element_type=F32)
               for h in range(H)]
        kvs = [jnp.dot(kt_scr[h, c], (dl_scr[h, c] * v_news[h]).astype(BF16),
                       preferred_element_type=F32) for h in range(H)]
        for h in range(H):
            state_scr[h] = sts[h] * gam_scr[h, c][0:1, :] + kvs[h]
            o_scr[h, c * C:(c + 1) * C, :] = egc_scr[h, c] * prs[h][C:] + ats[h]

    for h in range(H):
        o = o_scr[h]
        gate = gt_ref[h]
        y = o * _rms_scale(o) * on_ref[...] * _silu(gate)
        o_ref[:, h * DK:(h + 1) * DK] = y.astype(o_ref.dtype)


def _delta(p, ps, cw, arow, brow, onorm, *, batch, seq, tb):
    H = A_HEADS
    DK = A_DK
    T = batch * seq
    ns = seq // tb
    n_chunks = tb // A_CHUNK

    def sec_spec(sec):
        return pl.BlockSpec((H, tb, DK), lambda b, s: (sec, b * ns + s, 0))

    return pl.pallas_call(
        functools.partial(_delta_kernel, group=2),
        grid=(batch, ns),
        in_specs=[sec_spec(0), sec_spec(1), sec_spec(2), sec_spec(3),
                  pl.BlockSpec((tb, LANES), lambda b, s: (b * ns + s, 0)),
                  _const_spec((3, H, A_CONV, DK)),
                  _const_spec((1, LANES)), _const_spec((1, LANES)), _const_spec((1, DK))],
        out_specs=pl.BlockSpec((tb, H * DK), lambda b, s: (b * ns + s, 0)),
        out_shape=jax.ShapeDtypeStruct((T, H * DK), BF16),
        scratch_shapes=[
            pltpu.VMEM((H, DK, DK), F32),
            pltpu.VMEM((3, H, SUBLANES, DK), F32),
            pltpu.VMEM((tb + SUBLANES, DK), F32),
            pltpu.VMEM((2, H, tb, DK), F32),
            pltpu.VMEM((H, n_chunks, 2 * A_CHUNK, DK), BF16),
            pltpu.VMEM((H, n_chunks, DK, A_CHUNK), BF16),
            pltpu.VMEM((tb, LANES), F32),
            pltpu.VMEM((tb, LANES), F32),
            pltpu.VMEM((n_chunks, LANES, A_CHUNK), F32),
            pltpu.VMEM((H, n_chunks, A_CHUNK, A_CHUNK), BF16),
            pltpu.VMEM((H, n_chunks, A_CHUNK, DK), F32),
            pltpu.VMEM((H, n_chunks, A_CHUNK, DK), F32),
            pltpu.VMEM((H, n_chunks, A_CHUNK, DK), F32),
            pltpu.VMEM((H, n_chunks, SUBLANES, DK), F32),
            pltpu.VMEM((H, tb, DK), F32),
        ],
        compiler_params=_params(2),
        name="delta",
    )(p, p, p, p, ps, cw, arow, brow, onorm)


def _out_proj_kernel(x_ref, o_ref, w_ref, b_ref, h_ref):
    h_ref[...] = (x_ref[...] + b_ref[...]
                  + jnp.dot(o_ref[...], w_ref[...], preferred_element_type=F32))


def _out_proj(x2, o, w, b, *, tm):
    T, D = x2.shape
    K = o.shape[1]
    return pl.pallas_call(
        _out_proj_kernel,
        grid=(T // tm,),
        in_specs=[pl.BlockSpec((tm, D), lambda i: (i, 0)),
                  pl.BlockSpec((tm, K), lambda i: (i, 0)),
                  _const_spec((K, D)), _const_spec((1, D))],
        out_specs=pl.BlockSpec((tm, D), lambda i: (i, 0)),
        out_shape=jax.ShapeDtypeStruct((T, D), F32),
        compiler_params=_params(1),
        name="out_proj",
    )(x2, o, w, b)


def _ffn_kernel(x_ref, o_ref, wo_ref, bo_ref, g_ref, wu_ref, cw_ref, cb_ref, wd_ref, fg_ref,
                out_ref, halo_scr, up_scr, act_scr, *, tiles_per_seq, n_chunk, final_norm):
    i = pl.program_id(0)
    tm = x_ref.shape[0]
    F = wd_ref.shape[0]

    @pl.when(i % tiles_per_seq == 0)
    def _():
        halo_scr[...] = jnp.zeros_like(halo_scr)

    x = (x_ref[...] + bo_ref[...]
         + jnp.dot(o_ref[...], wo_ref[...], preferred_element_type=F32))
    hn = (x * _rms_scale(x) * g_ref[...]).astype(BF16)

    def conv_cols(c0):
        u = jnp.dot(hn, wu_ref[:, c0:c0 + n_chunk], preferred_element_type=F32)
        up_scr[0:SUBLANES, :] = halo_scr[:, c0:c0 + n_chunk]
        up_scr[SUBLANES:SUBLANES + tm, :] = u
        halo_scr[:, c0:c0 + n_chunk] = u[tm - SUBLANES:, :]
        w = cw_ref[:, c0:c0 + n_chunk]
        y = u * w[FFN_CONV - 1:FFN_CONV, :] + cb_ref[:, c0:c0 + n_chunk]
        for j in range(FFN_CONV - 1):
            off = SUBLANES - (FFN_CONV - 1) + j
            y = y + up_scr[off:off + tm, :] * w[j:j + 1, :]
        return y

    for c in range(F // n_chunk):
        gate = conv_cols(c * n_chunk)
        up = conv_cols(F + c * n_chunk)
        act_scr[:, c * n_chunk:(c + 1) * n_chunk] = (_silu(gate) * up).astype(BF16)

    out = x + jnp.dot(act_scr[...], wd_ref[...], preferred_element_type=F32)
    if final_norm:
        out = out * _rms_scale(out) * fg_ref[...]
    out_ref[...] = out


def _ffn(h2, o, w_o, b_o, g, w_up, conv_w, conv_b, w_down, final_g, *, seq, tm, final_norm):
    T, D = h2.shape
    F = w_down.shape[0]
    KO = o.shape[1]
    n_chunk = 256
    return pl.pallas_call(
        functools.partial(_ffn_kernel, tiles_per_seq=seq // tm, n_chunk=n_chunk,
                          final_norm=final_norm),
        grid=(T // tm,),
        in_specs=[pl.BlockSpec((tm, D), lambda i: (i, 0)),
                  pl.BlockSpec((tm, KO), lambda i: (i, 0)),
                  _const_spec((KO, D)),
                  _const_spec((1, D)),
                  _const_spec((1, D)),
                  _const_spec((D, 2 * F)),
                  _const_spec((FFN_CONV, 2 * F)),
                  _const_spec((1, 2 * F)),
                  _const_spec((F, D)),
                  _const_spec((1, D))],
        out_specs=pl.BlockSpec((tm, D), lambda i: (i, 0)),
        out_shape=jax.ShapeDtypeStruct((T, D), F32),
        scratch_shapes=[
            pltpu.VMEM((SUBLANES, 2 * F), F32),
            pltpu.VMEM((tm + SUBLANES, n_chunk), F32),
            pltpu.VMEM((tm, F), BF16),
        ],
        compiler_params=_params(1),
        name="ffn",
    )(h2, o, w_o, b_o, g, w_up, conv_w, conv_b, w_down, final_g)


def _qkv_proj_kernel(x_ref, gkv_ref, wkv_ref, bkv_ref, gq_ref, wq_ref, bq_ref, kv_ref, q_ref):
    x = x_ref[...]
    xn = x * _rms_scale(x)
    kv = jnp.dot((xn * gkv_ref[...]).astype(BF16), wkv_ref[...], preferred_element_type=F32)
    kv_ref[...] = (kv + bkv_ref[...]).astype(kv_ref.dtype)
    q = jnp.dot((xn * gq_ref[...]).astype(BF16), wq_ref[...], preferred_element_type=F32)
    q_ref[...] = ((q + bq_ref[...]) * (B_DH ** -0.5)).astype(q_ref.dtype)


def _qkv_proj(h2, gkv, wkv, bkv, gq, wq, bq, *, tm):
    T, D = h2.shape
    NKV = wkv.shape[1]
    NQ = wq.shape[1]
    return pl.pallas_call(
        _qkv_proj_kernel,
        grid=(T // tm,),
        in_specs=[pl.BlockSpec((tm, D), lambda i: (i, 0)),
                  _const_spec((1, D)), _const_spec((D, NKV)), _const_spec((1, NKV)),
                  _const_spec((1, D)), _const_spec((D, NQ)), _const_spec((1, NQ))],
        out_specs=[pl.BlockSpec((tm, NKV), lambda i: (i, 0)),
                   pl.BlockSpec((tm, NQ), lambda i: (i, 0))],
        out_shape=[jax.ShapeDtypeStruct((T, NKV), BF16),
                   jax.ShapeDtypeStruct((T, NQ), BF16)],
        compiler_params=_params(1),
        name="qkv_proj",
    )(h2, gkv, wkv, bkv, gq, wq, bq)


def _swa_kernel(sink_ref, q_ref, kvp_ref, kvc_ref, o_ref):
    j = pl.program_id(1)
    W = B_WINDOW
    TQ = q_ref.shape[0]
    nkv = B_KV_HEADS * B_DH
    per_tile = LANES // B_DH
    ri = lax.broadcasted_iota(jnp.int32, (W, W), 0)
    ci = lax.broadcasted_iota(jnp.int32, (W, W), 1)
    upper = ci > ri
    dist = jnp.where(upper, ri - ci + W, ri - ci).astype(F32)
    no_prev = jnp.logical_and(upper, j == 0)
    assert per_tile == 2
    nt = (((1,), (1,)), ((), ()))

    def window(r0, col):
        cur = kvc_ref[r0:r0 + W, col:col + LANES]
        prev = kvp_ref[:, col:col + LANES] if r0 == 0 else kvc_ref[r0 - W:r0, col:col + LANES]
        return jnp.concatenate([prev, cur], axis=0)

    lo_kv = lax.broadcasted_iota(jnp.int32, (2 * W, LANES), 1) < B_DH
    lo_q = lax.broadcasted_iota(jnp.int32, (W, LANES), 1) < B_DH

    def both_halves(x, half):
        swapped = jnp.concatenate([x[:, B_DH:], x[:, :B_DH]], axis=1)
        return jnp.where(lo_kv, x, swapped) if half == 0 else jnp.where(lo_kv, swapped, x)

    def keep_half(x, lo_mask, e):
        return jnp.where(lo_mask, x, jnp.zeros_like(x)) if e == 0 else jnp.where(lo_mask, jnp.zeros_like(x), x)

    def scores(unit):
        sb, kh = unit
        r0 = sb * W
        tile, half = divmod(kh, per_tile)
        kd = both_halves(window(r0, tile * LANES), half)
        out = []
        for pair in range(B_GROUP // per_tile):
            h0 = kh * B_GROUP + pair * per_tile
            qp = q_ref[r0:r0 + W, h0 * B_DH:h0 * B_DH + LANES]
            out.append([lax.dot_general(keep_half(qp, lo_q, e), kd, nt, preferred_element_type=F32)
                        for e in range(per_tile)])
        return out

    def finish(unit, s2s):
        sb, kh = unit
        r0 = sb * W
        tile, half = divmod(kh, per_tile)
        vd = both_halves(window(r0, nkv + tile * LANES), half)
        v_pad = [keep_half(vd, lo_kv, e) for e in range(per_tile)]
        p2s = []
        for pair in range(B_GROUP // per_tile):
            h0 = kh * B_GROUP + pair * per_tile
            for e in range(per_tile):
                head = h0 + e
                slope = 2.0 ** (-8.0 * (head + 1) / B_HEADS)
                s2 = s2s[pair][e]
                s = jnp.where(upper, s2[:, :W], s2[:, W:]) - slope * dist
                if sb == 0:
                    s = jnp.where(no_prev, -jnp.inf, s)
                sink = sink_ref[head]
                m = jnp.maximum(jnp.max(s, axis=-1, keepdims=True), sink)
                p = jnp.exp(s - m)
                denom = jnp.sum(p, axis=-1, keepdims=True) + jnp.exp(sink - m)
                p = p * (1.0 / denom)
                p2s.append(jnp.concatenate([jnp.where(upper, p, 0.0), jnp.where(upper, 0.0, p)],
                                           axis=1).astype(BF16))
        for pair in range(B_GROUP // per_tile):
            h0 = kh * B_GROUP + pair * per_tile
            acc = None
            for e in range(per_tile):
                part = jnp.dot(p2s[pair * per_tile + e], v_pad[e],
                               preferred_element_type=F32)
                acc = part if acc is None else acc + part
            o_ref[r0:r0 + W, h0 * B_DH:h0 * B_DH + LANES] = acc.astype(o_ref.dtype)

    units = [(sb, kh) for sb in range(TQ // W) for kh in range(B_KV_HEADS)]
    ahead = 1
    pending = [scores(unit) for unit in units[:ahead]]
    for u, unit in enumerate(units):
        if u + ahead < len(units):
            pending.append(scores(units[u + ahead]))
        finish(unit, pending.pop(0))


def _swa(q, kv, sinks, *, batch, seq, tq):
    T, NQ = q.shape
    NKV2 = kv.shape[1]
    W = B_WINDOW
    nq = seq // tq
    r = tq // W
    return pl.pallas_call(
        _swa_kernel,
        grid=(batch, nq),
        in_specs=[pl.BlockSpec(memory_space=pltpu.SMEM),
                  pl.BlockSpec((tq, NQ), lambda b, j: (b * nq + j, 0)),
                  pl.BlockSpec((W, NKV2), lambda b, j: (jnp.maximum((b * nq + j) * r - 1, 0), 0)),
                  pl.BlockSpec((tq, NKV2), lambda b, j: (b * nq + j, 0))],
        out_specs=pl.BlockSpec((tq, NQ), lambda b, j: (b * nq + j, 0)),
        out_shape=jax.ShapeDtypeStruct((T, NQ), BF16),
        compiler_params=_params(2),
        name="swa",
    )(sinks, q, kv, kv)


def _row(v):
    return v.reshape(1, -1).astype(F32)


def _delta_mixer(h2, norm_g, w_in, conv_w, a_log, dt_bias, onorm, *, batch, seq):
    H = A_HEADS
    n_main = 4 * H * A_DK
    w_main = w_in[:, :n_main].astype(BF16)
    w_small = jnp.pad(w_in[:, n_main:], ((0, 0), (0, LANES - 2 * H))).astype(BF16)
    p, ps = _in_proj(h2, _row(norm_g), w_main, w_small, tm=256)
    cw = conv_w.reshape(A_CONV, 3, H, A_DK).transpose(1, 2, 0, 3)
    arow = jnp.pad(a_log, (H, LANES - 2 * H)).reshape(1, LANES)
    brow = jnp.pad(dt_bias, (H, LANES - 2 * H)).reshape(1, LANES)
    return _delta(p, ps, cw, arow, brow, _row(onorm), batch=batch, seq=seq, tb=256)


def kernel(x, a_norm, a_w_in, a_conv_w, a_A_log, a_dt_bias, a_onorm, a_w_out, kv_norm, kv_w, kv_b, b_norm, b_w_q, b_b_q, b_sinks, b_w_o, b_b_o, f_norm, f_w_up, f_conv_w, f_conv_b, f_w_down, final_norm):
    B, S, D = x.shape
    depth = f_norm.shape[0]
    n_a = a_norm.shape[0]
    h = x.reshape(B * S, D)
    kv = None
    for layer in range(depth):
        if layer < n_a:
            i = layer
            o = _delta_mixer(h, a_norm[i], a_w_in[i], a_conv_w[i], a_A_log[i], a_dt_bias[i],
                             a_onorm[i], batch=B, seq=S)
            w_o, b_o = a_w_out[i], jnp.zeros((1, D), F32)
        else:
            i = layer - n_a
            gkv = kv_norm if i == 0 else b_norm[i]
            kv_new, q = _qkv_proj(h, _row(gkv), kv_w.astype(BF16), _row(kv_b),
                                  _row(b_norm[i]), b_w_q[i].astype(BF16), _row(b_b_q[i]), tm=512)
            if i == 0:
                kv = kv_new
            o = _swa(q, kv, b_sinks[i].astype(F32), batch=B, seq=S, tq=512)
            w_o, b_o = b_w_o[i], _row(b_b_o[i])
        h = _ffn(h, o, w_o.astype(BF16), b_o, _row(f_norm[layer]), f_w_up[layer].astype(BF16),
                 f_conv_w[layer], _row(f_conv_b[layer]), f_w_down[layer].astype(BF16),
                 _row(final_norm), seq=S, tm=512, final_norm=(layer == depth - 1))
    return h.reshape(B, S, D)
```

```python
import functools

import jax
import jax.numpy as jnp
from jax import lax
from jax.experimental import pallas as pl
from jax.experimental.pallas import tpu as pltpu

F32 = jnp.float32
BF16 = jnp.bfloat16
EPS = 1e-6

LANES = 128
SUBLANES = 8
VMEM_LIMIT = 56 * 1024 * 1024

A_HEADS = 8
A_DK = 128
A_CONV = 4
A_CHUNK = 64

B_HEADS = 16
B_KV_HEADS = 4
B_GROUP = B_HEADS // B_KV_HEADS
B_DH = 64
B_WINDOW = 128

FFN_CONV = 3


def _params(n_axes, vmem=VMEM_LIMIT):
    return pltpu.CompilerParams(dimension_semantics=("arbitrary",) * n_axes,
                                vmem_limit_bytes=vmem)


def _const_spec(shape):
    nd = len(shape)
    return pl.BlockSpec(shape, lambda *_: (0,) * nd, pipeline_mode=pl.Buffered(1))


def _rms_scale(x):
    return lax.rsqrt(jnp.mean(x * x, axis=-1, keepdims=True) + EPS)


def _sigmoid(x):
    return 1.0 / (1.0 + jnp.exp(-x))


def _silu(x):
    return x * _sigmoid(x)


def _softplus(x):
    return jnp.maximum(x, 0.0) + jnp.log(1.0 + jnp.exp(-jnp.abs(x)))


def _in_proj_kernel(x_ref, g_ref, w_ref, ws_ref, cw_ref, q_ref, k_ref, v_ref, gt_ref, ps_ref,
                    xp_scr, *, n_chunk, tiles_per_seq):
    i = pl.program_id(0)
    tm = x_ref.shape[0]
    H, _, DK = q_ref.shape
    per = n_chunk // LANES

    @pl.when(i % tiles_per_seq == 0)
    def _():
        xp_scr[:, 0:SUBLANES, :] = jnp.zeros((xp_scr.shape[0], SUBLANES, LANES), F32)

    x = x_ref[...]
    hn = (x * _rms_scale(x) * g_ref[...]).astype(BF16)
    def project(c):
        return jnp.dot(hn, w_ref[:, c * n_chunk:(c + 1) * n_chunk], preferred_element_type=F32)

    n_chunks = 4 * H // per
    nxt = project(0)
    for c in range(n_chunks):
        res = nxt
        if c + 1 < n_chunks:
            nxt = project(c + 1)
        for j in range(per):
            slab = c * per + j
            sec, h = divmod(slab, H)
            u = res[:, j * LANES:(j + 1) * LANES]
            if sec == 3:
                gt_ref[h] = u
                continue
            xp_scr[slab, SUBLANES:SUBLANES + tm, :] = u
            w = cw_ref[sec, h]
            y = u * w[A_CONV - 1:A_CONV, :]
            for t in range(A_CONV - 1):
                off = SUBLANES - (A_CONV - 1) + t
                y = y + xp_scr[slab, off:off + tm, :] * w[t:t + 1, :]
            xp_scr[slab, 0:SUBLANES, :] = u[tm - SUBLANES:, :]
            y = _silu(y)
            if sec == 0:
                scale = lax.rsqrt(jnp.sum(y * y, axis=-1, keepdims=True) + EPS) * (DK ** -0.5)
                q_ref[h] = (y * scale).astype(q_ref.dtype)
            elif sec == 1:
                k_ref[h] = y * lax.rsqrt(jnp.sum(y * y, axis=-1, keepdims=True) + EPS)
            else:
                v_ref[h] = y
    ps_ref[...] = jnp.dot(hn, ws_ref[...], preferred_element_type=F32)


def _in_proj(x2, g, w_main, w_small, cw, *, seq, tm):
    T, D = x2.shape
    N = w_main.shape[1]
    H = A_HEADS
    head_major = pl.BlockSpec((H, tm, A_DK), lambda i: (0, i, 0))
    return pl.pallas_call(
        functools.partial(_in_proj_kernel, n_chunk=512, tiles_per_seq=seq // tm),
        grid=(T // tm,),
        in_specs=[pl.BlockSpec((tm, D), lambda i: (i, 0)),
                  _const_spec((1, D)),
                  _const_spec((D, N)),
                  _const_spec((D, LANES)),
                  _const_spec((3, H, A_CONV, A_DK))],
        out_specs=[head_major, head_major, head_major, head_major,
                   pl.BlockSpec((tm, LANES), lambda i: (i, 0))],
        out_shape=[jax.ShapeDtypeStruct((H, T, A_DK), BF16),
                   jax.ShapeDtypeStruct((H, T, A_DK), F32),
                   jax.ShapeDtypeStruct((H, T, A_DK), F32),
                   jax.ShapeDtypeStruct((H, T, A_DK), F32),
                   jax.ShapeDtypeStruct((T, LANES), F32)],
        scratch_shapes=[pltpu.VMEM((3 * H, tm + SUBLANES, LANES), F32)],
        compiler_params=_params(1),
        name="in_proj",
    )(x2, g, w_main, w_small, cw)


def _unit_lower_inverses(lmats):
    n = lmats[0].shape[0]
    n_tiles = n // SUBLANES
    lane = lax.broadcasted_iota(jnp.int32, (SUBLANES, n), 1)
    sub = lax.broadcasted_iota(jnp.int32, (SUBLANES, n), 0)
    blk = lane // SUBLANES
    base = blk * SUBLANES
    eye = jnp.where(lane - base == sub, 1.0, 0.0).astype(F32)
    zero = jnp.zeros((SUBLANES, n), F32)
    lts = [[lm[t * SUBLANES:(t + 1) * SUBLANES, :] for t in range(n_tiles)] for lm in lmats]
    tts = []
    for lt in lts:
        ld = lt[n_tiles - 1]
        for t in reversed(range(n_tiles - 1)):
            ld = jnp.where(blk == t, lt[t], ld)
        td = eye
        for m in range(SUBLANES - 1):
            col = jnp.take_along_axis(ld, base + m, axis=1)
            td = td - col * td[m:m + 1, :]
        tts.append([jnp.where(blk == t, td, 0.0) for t in range(n_tiles)])
    s = SUBLANES
    while s < n:
        odd = [t for t in range(n_tiles) if (t * SUBLANES // s) % 2 == 1]
        ws = []
        for lt, tt in zip(lts, tts):
            l_odd = jnp.concatenate(
                [jnp.where(lane // s == t * SUBLANES // s - 1, lt[t], 0.0) for t in odd], axis=0)
            t_full = jnp.concatenate(tt, axis=0).astype(BF16)
            ws.append(jnp.dot(l_odd.astype(BF16), t_full, preferred_element_type=F32))
        xs = []
        for tt, w in zip(tts, ws):
            w_rows = {t: w[i * SUBLANES:(i + 1) * SUBLANES, :] for i, t in enumerate(odd)}
            w_full = jnp.concatenate([w_rows.get(t, zero) for t in range(n_tiles)], axis=0)
            t_odd = jnp.concatenate([tt[t] for t in odd], axis=0).astype(BF16)
            xs.append(jnp.dot(t_odd, w_full.astype(BF16), preferred_element_type=F32))
        for tt, x in zip(tts, xs):
            for i, t in enumerate(odd):
                tt[t] = tt[t] - x[i * SUBLANES:(i + 1) * SUBLANES, :]
        s *= 2
    return [jnp.concatenate(tt, axis=0).astype(BF16) for tt in tts]


def _delta_kernel(q_ref, k_ref, v_ref, gt_ref, ps_ref, arow_ref, brow_ref, on_ref,
                  o_ref,
                  state_scr, kq_scr, kt_scr, gc_scr, beta_scr, grow_scr,
                  attn_scr, val_scr, egc_scr, dl_scr, gam_scr, o_scr, *, group):
    s = pl.program_id(1)
    H, Tb, DK = q_ref.shape
    C = A_CHUNK
    n_chunks = Tb // C

    @pl.when(s == 0)
    def _():
        state_scr[...] = jnp.zeros_like(state_scr)

    for h in range(H):
        qb = q_ref[h]
        k = k_ref[h]
        kb = k.astype(BF16)
        ktb = k.T.astype(BF16)
        for c in range(n_chunks):
            kq_scr[h, c, 0:C, :] = kb[c * C:(c + 1) * C, :]
            kq_scr[h, c, C:2 * C, :] = qb[c * C:(c + 1) * C, :]
            kt_scr[h, c] = ktb[:, c * C:(c + 1) * C]

    ps = ps_ref[...]
    beta_scr[...] = _sigmoid(ps)
    g = -jnp.exp(arow_ref[...]) * _softplus(ps + brow_ref[...])
    pos = lax.broadcasted_iota(jnp.int32, (Tb, LANES), 0) % C
    gc = g
    sh = 1
    while sh < C:
        gc = gc + jnp.where(pos >= sh, pltpu.roll(gc, sh, 0), 0.0)
        sh *= 2
    gc_scr[...] = gc
    gct = gc.T
    for c in range(n_chunks):
        grow_scr[c] = gct[:, c * C:(c + 1) * C]

    ri = lax.broadcasted_iota(jnp.int32, (C, C), 0)
    ci = lax.broadcasted_iota(jnp.int32, (C, C), 1)
    incl = ri >= ci
    strict = ri > ci

    def gate_cols(c, h):
        gcol = jnp.broadcast_to(gc_scr[c * C:(c + 1) * C, H + h:H + h + 1], (C, DK))
        bcol = jnp.broadcast_to(beta_scr[c * C:(c + 1) * C, h:h + 1], (C, DK))
        return gcol, bcol

    for c0 in range(0, n_chunks, group):
        chains = [(c, h) for c in range(c0, c0 + group) for h in range(H)]
        scs = [jnp.dot(kq_scr[h, c], kt_scr[h, c], preferred_element_type=F32)
               for c, h in chains]
        lmats = []
        for (c, h), sc in zip(chains, scs):
            gcol, bcol = gate_cols(c, h)
            grow = grow_scr[c, H + h:H + h + 1, :]
            decay = jnp.exp(jnp.where(incl, gcol[:, :C] - grow, -jnp.inf))
            lmats.append(jnp.where(strict, sc[:C] * (decay * bcol[:, :C]), 0.0))
            attn_scr[h, c] = (sc[C:] * decay).astype(BF16)
        tmats = _unit_lower_inverses(lmats)
        sols = []
        for (c, h), tmat in zip(chains, tmats):
            gcol, bcol = gate_cols(c, h)
            glast = gcol[C - 1:C, :]
            egc = jnp.exp(gcol)
            k = k_ref[h, c * C:(c + 1) * C, :]
            v = v_ref[h, c * C:(c + 1) * C, :]
            rhs = jnp.concatenate([k * (bcol * egc), v * bcol], axis=1).astype(BF16)
            sols.append(jnp.dot(tmat, rhs, preferred_element_type=F32))
            egc_scr[h, c] = egc
            dl_scr[h, c] = jnp.exp(glast - gcol)
            gam_scr[h, c] = jnp.broadcast_to(jnp.exp(glast), (SUBLANES, DK))
        for (c, h), sol in zip(chains, sols):
            kq_scr[h, c, 0:C, :] = sol[:, :DK].astype(BF16)
            val_scr[h, c] = sol[:, DK:]

    for c in range(n_chunks):
        sts = [state_scr[h] for h in range(H)]
        prs = [jnp.dot(kq_scr[h, c], sts[h].astype(BF16), preferred_element_type=F32)
               for h in range(H)]
        v_news = [val_scr[h, c] - prs[h][:C] for h in range(H)]
        ats = [jnp.dot(attn_scr[h, c], v_news[h].astype(BF16), preferred_element_type=F32)
               for h in range(H)]
        kvs = [jnp.dot(kt_scr[h, c], (dl_scr[h, c] * v_news[h]).astype(BF16),
                       preferred_element_type=F32) for h in range(H)]
        for h in range(H):
            state_scr[h] = sts[h] * gam_scr[h, c][0:1, :] + kvs[h]
            o_scr[h, c * C:(c + 1) * C, :] = egc_scr[h, c] * prs[h][C:] + ats[h]

    for h in range(H):
        o = o_scr[h]
        gate = gt_ref[h]
        y = o * _rms_scale(o) * on_ref[...] * _silu(gate)
        o_ref[:, h * DK:(h + 1) * DK] = y.astype(o_ref.dtype)


def _delta(q, k, v, gate, ps, arow, brow, onorm, *, batch, seq, tb):
    H = A_HEADS
    DK = A_DK
    T = batch * seq
    ns = seq // tb
    n_chunks = tb // A_CHUNK
    head_major = pl.BlockSpec((H, tb, DK), lambda b, s: (0, b * ns + s, 0))

    return pl.pallas_call(
        functools.partial(_delta_kernel, group=n_chunks),
        grid=(batch, ns),
        in_specs=[head_major, head_major, head_major, head_major,
                  pl.BlockSpec((tb, LANES), lambda b, s: (b * ns + s, 0)),
                  _const_spec((1, LANES)), _const_spec((1, LANES)), _const_spec((1, DK))],
        out_specs=pl.BlockSpec((tb, H * DK), lambda b, s: (b * ns + s, 0)),
        out_shape=jax.ShapeDtypeStruct((T, H * DK), BF16),
        scratch_shapes=[
            pltpu.VMEM((H, DK, DK), F32),
            pltpu.VMEM((H, n_chunks, 2 * A_CHUNK, DK), BF16),
            pltpu.VMEM((H, n_chunks, DK, A_CHUNK), BF16),
            pltpu.VMEM((tb, LANES), F32),
            pltpu.VMEM((tb, LANES), F32),
            pltpu.VMEM((n_chunks, LANES, A_CHUNK), F32),
            pltpu.VMEM((H, n_chunks, A_CHUNK, A_CHUNK), BF16),
            pltpu.VMEM((H, n_chunks, A_CHUNK, DK), F32),
            pltpu.VMEM((H, n_chunks, A_CHUNK, DK), F32),
            pltpu.VMEM((H, n_chunks, A_CHUNK, DK), F32),
            pltpu.VMEM((H, n_chunks, SUBLANES, DK), F32),
            pltpu.VMEM((H, tb, DK), F32),
        ],
        compiler_params=_params(2),
        name="delta",
    )(q, k, v, gate, ps, arow, brow, onorm)


def _ffn_kernel(x_ref, o_ref, wo_ref, bo_ref, g_ref, wu_ref, cw_ref, cb_ref, wd_ref, fg_ref,
                out_ref, halo_scr, up_scr, act_scr, *, tiles_per_seq, n_chunk, final_norm):
    i = pl.program_id(0)
    tm = x_ref.shape[0]
    F = wd_ref.shape[0]

    @pl.when(i % tiles_per_seq == 0)
    def _():
        halo_scr[...] = jnp.zeros_like(halo_scr)

    x = (x_ref[...] + bo_ref[...]
         + jnp.dot(o_ref[...], wo_ref[...], preferred_element_type=F32))
    hn = (x * _rms_scale(x) * g_ref[...]).astype(BF16)

    def conv_cols(c0):
        u = jnp.dot(hn, wu_ref[:, c0:c0 + n_chunk], preferred_element_type=F32)
        up_scr[0:SUBLANES, :] = halo_scr[:, c0:c0 + n_chunk]
        up_scr[SUBLANES:SUBLANES + tm, :] = u
        halo_scr[:, c0:c0 + n_chunk] = u[tm - SUBLANES:, :]
        w = cw_ref[:, c0:c0 + n_chunk]
        y = u * w[FFN_CONV - 1:FFN_CONV, :] + cb_ref[:, c0:c0 + n_chunk]
        for j in range(FFN_CONV - 1):
            off = SUBLANES - (FFN_CONV - 1) + j
            y = y + up_scr[off:off + tm, :] * w[j:j + 1, :]
        return y

    for c in range(F // n_chunk):
        gate = conv_cols(c * n_chunk)
        up = conv_cols(F + c * n_chunk)
        act_scr[:, c * n_chunk:(c + 1) * n_chunk] = (_silu(gate) * up).astype(BF16)

    out = x + jnp.dot(act_scr[...], wd_ref[...], preferred_element_type=F32)
    if final_norm:
        out = out * _rms_scale(out) * fg_ref[...]
    out_ref[...] = out


def _ffn(h2, o, w_o, b_o, g, w_up, conv_w, conv_b, w_down, final_g, *, seq, tm, final_norm):
    T, D = h2.shape
    F = w_down.shape[0]
    KO = o.shape[1]
    n_chunk = 256
    return pl.pallas_call(
        functools.partial(_ffn_kernel, tiles_per_seq=seq // tm, n_chunk=n_chunk,
                          final_norm=final_norm),
        grid=(T // tm,),
        in_specs=[pl.BlockSpec((tm, D), lambda i: (i, 0)),
                  pl.BlockSpec((tm, KO), lambda i: (i, 0)),
                  _const_spec((KO, D)),
                  _const_spec((1, D)),
                  _const_spec((1, D)),
                  _const_spec((D, 2 * F)),
                  _const_spec((FFN_CONV, 2 * F)),
                  _const_spec((1, 2 * F)),
                  _const_spec((F, D)),
                  _const_spec((1, D))],
        out_specs=pl.BlockSpec((tm, D), lambda i: (i, 0)),
        out_shape=jax.ShapeDtypeStruct((T, D), F32),
        scratch_shapes=[
            pltpu.VMEM((SUBLANES, 2 * F), F32),
            pltpu.VMEM((tm + SUBLANES, n_chunk), F32),
            pltpu.VMEM((tm, F), BF16),
        ],
        compiler_params=_params(1),
        name="ffn",
    )(h2, o, w_o, b_o, g, w_up, conv_w, conv_b, w_down, final_g)


def _qkv_proj_kernel(x_ref, gkv_ref, wkv_ref, bkv_ref, gq_ref, wq_ref, bq_ref, kv_ref, q_ref):
    x = x_ref[...]
    xn = x * _rms_scale(x)
    kv = jnp.dot((xn * gkv_ref[...]).astype(BF16), wkv_ref[...], preferred_element_type=F32)
    kv_ref[...] = (kv + bkv_ref[...]).astype(kv_ref.dtype)
    q = jnp.dot((xn * gq_ref[...]).astype(BF16), wq_ref[...], preferred_element_type=F32)
    q_ref[...] = ((q + bq_ref[...]) * (B_DH ** -0.5)).astype(q_ref.dtype)


def _qkv_proj(h2, gkv, wkv, bkv, gq, wq, bq, *, tm):
    T, D = h2.shape
    NKV = wkv.shape[1]
    NQ = wq.shape[1]
    return pl.pallas_call(
        _qkv_proj_kernel,
        grid=(T // tm,),
        in_specs=[pl.BlockSpec((tm, D), lambda i: (i, 0)),
                  _const_spec((1, D)), _const_spec((D, NKV)), _const_spec((1, NKV)),
                  _const_spec((1, D)), _const_spec((D, NQ)), _const_spec((1, NQ))],
        out_specs=[pl.BlockSpec((tm, NKV), lambda i: (i, 0)),
                   pl.BlockSpec((tm, NQ), lambda i: (i, 0))],
        out_shape=[jax.ShapeDtypeStruct((T, NKV), BF16),
                   jax.ShapeDtypeStruct((T, NQ), BF16)],
        compiler_params=_params(1),
        name="qkv_proj",
    )(h2, gkv, wkv, bkv, gq, wq, bq)


def _swa_kernel(sink_ref, q_ref, kvp_ref, kvc_ref, o_ref):
    j = pl.program_id(1)
    W = B_WINDOW
    TQ = q_ref.shape[0]
    nkv = B_KV_HEADS * B_DH
    per_tile = LANES // B_DH
    ri = lax.broadcasted_iota(jnp.int32, (W, W), 0)
    ci = lax.broadcasted_iota(jnp.int32, (W, W), 1)
    upper = ci > ri
    dist = jnp.where(upper, ri - ci + W, ri - ci).astype(F32)
    no_prev = jnp.logical_and(upper, j == 0)
    assert per_tile == 2
    nt = (((1,), (1,)), ((), ()))

    def window(r0, col):
        cur = kvc_ref[r0:r0 + W, col:col + LANES]
        prev = kvp_ref[:, col:col + LANES] if r0 == 0 else kvc_ref[r0 - W:r0, col:col + LANES]
        return jnp.concatenate([prev, cur], axis=0)

    lo_kv = lax.broadcasted_iota(jnp.int32, (2 * W, LANES), 1) < B_DH
    lo_q = lax.broadcasted_iota(jnp.int32, (W, LANES), 1) < B_DH

    def both_halves(x, half):
        swapped = jnp.concatenate([x[:, B_DH:], x[:, :B_DH]], axis=1)
        return jnp.where(lo_kv, x, swapped) if half == 0 else jnp.where(lo_kv, swapped, x)

    def keep_half(x, lo_mask, e):
        return jnp.where(lo_mask, x, jnp.zeros_like(x)) if e == 0 else jnp.where(lo_mask, jnp.zeros_like(x), x)

    def scores(unit):
        sb, kh = unit
        r0 = sb * W
        tile, half = divmod(kh, per_tile)
        kd = both_halves(window(r0, tile * LANES), half)
        out = []
        for pair in range(B_GROUP // per_tile):
            h0 = kh * B_GROUP + pair * per_tile
            qp = q_ref[r0:r0 + W, h0 * B_DH:h0 * B_DH + LANES]
            out.append([lax.dot_general(keep_half(qp, lo_q, e), kd, nt, preferred_element_type=F32)
                        for e in range(per_tile)])
        return out

    def finish(unit, s2s):
        sb, kh = unit
        r0 = sb * W
        tile, half = divmod(kh, per_tile)
        vd = both_halves(window(r0, nkv + tile * LANES), half)
        v_pad = [keep_half(vd, lo_kv, e) for e in range(per_tile)]
        p2s = []
        for pair in range(B_GROUP // per_tile):
            h0 = kh * B_GROUP + pair * per_tile
            for e in range(per_tile):
                head = h0 + e
                slope = 2.0 ** (-8.0 * (head + 1) / B_HEADS)
                s2 = s2s[pair][e]
                s = jnp.where(upper, s2[:, :W], s2[:, W:]) - slope * dist
                if sb == 0:
                    s = jnp.where(no_prev, -jnp.inf, s)
                sink = sink_ref[head]
                m = jnp.maximum(jnp.max(s, axis=-1, keepdims=True), sink)
                p = jnp.exp(s - m)
                denom = jnp.sum(p, axis=-1, keepdims=True) + jnp.exp(sink - m)
                p = p * (1.0 / denom)
                p2s.append(jnp.concatenate([jnp.where(upper, p, 0.0), jnp.where(upper, 0.0, p)],
                                           axis=1).astype(BF16))
        for pair in range(B_GROUP // per_tile):
            h0 = kh * B_GROUP + pair * per_tile
            acc = None
            for e in range(per_tile):
                part = jnp.dot(p2s[pair * per_tile + e], v_pad[e],
                               preferred_element_type=F32)
                acc = part if acc is None else acc + part
            o_ref[r0:r0 + W, h0 * B_DH:h0 * B_DH + LANES] = acc.astype(o_ref.dtype)

    units = [(sb, kh) for sb in range(TQ // W) for kh in range(B_KV_HEADS)]
    ahead = 1
    pending = [scores(unit) for unit in units[:ahead]]
    for u, unit in enumerate(units):
        if u + ahead < len(units):
            pending.append(scores(units[u + ahead]))
        finish(unit, pending.pop(0))


def _swa(q, kv, sinks, *, batch, seq, tq):
    T, NQ = q.shape
    NKV2 = kv.shape[1]
    W = B_WINDOW
    nq = seq // tq
    r = tq // W
    return pl.pallas_call(
        _swa_kernel,
        grid=(batch, nq),
        in_specs=[pl.BlockSpec(memory_space=pltpu.SMEM),
                  pl.BlockSpec((tq, NQ), lambda b, j: (b * nq + j, 0)),
                  pl.BlockSpec((W, NKV2), lambda b, j: (jnp.maximum((b * nq + j) * r - 1, 0), 0)),
                  pl.BlockSpec((tq, NKV2), lambda b, j: (b * nq + j, 0))],
        out_specs=pl.BlockSpec((tq, NQ), lambda b, j: (b * nq + j, 0)),
        out_shape=jax.ShapeDtypeStruct((T, NQ), BF16),
        compiler_params=_params(2),
        name="swa",
    )(sinks, q, kv, kv)


def _row(v):
    return v.reshape(1, -1).astype(F32)


def _delta_mixer(h2, norm_g, w_in, conv_w, a_log, dt_bias, onorm, *, batch, seq):
    H = A_HEADS
    n_main = 4 * H * A_DK
    w_main = w_in[:, :n_main].astype(BF16)
    w_small = jnp.pad(w_in[:, n_main:], ((0, 0), (0, LANES - 2 * H))).astype(BF16)
    cw = conv_w.reshape(A_CONV, 3, H, A_DK).transpose(1, 2, 0, 3)
    q, k, v, gate, ps = _in_proj(h2, _row(norm_g), w_main, w_small, cw, seq=seq, tm=256)
    arow = jnp.pad(a_log, (H, LANES - 2 * H)).reshape(1, LANES)
    brow = jnp.pad(dt_bias, (H, LANES - 2 * H)).reshape(1, LANES)
    return _delta(q, k, v, gate, ps, arow, brow, _row(onorm), batch=batch, seq=seq, tb=256)


def kernel(x, a_norm, a_w_in, a_conv_w, a_A_log, a_dt_bias, a_onorm, a_w_out, kv_norm, kv_w, kv_b, b_norm, b_w_q, b_b_q, b_sinks, b_w_o, b_b_o, f_norm, f_w_up, f_conv_w, f_conv_b, f_w_down, final_norm):
    B, S, D = x.shape
    depth = f_norm.shape[0]
    n_a = a_norm.shape[0]
    h = x.reshape(B * S, D)
    kv = None
    for layer in range(depth):
        if layer < n_a:
            i = layer
            o = _delta_mixer(h, a_norm[i], a_w_in[i], a_conv_w[i], a_A_log[i], a_dt_bias[i],
                             a_onorm[i], batch=B, seq=S)
            w_o, b_o = a_w_out[i], jnp.zeros((1, D), F32)
        else:
            i = layer - n_a
            gkv = kv_norm if i == 0 else b_norm[i]
            kv_new, q = _qkv_proj(h, _row(gkv), kv_w.astype(BF16), _row(kv_b),
                                  _row(b_norm[i]), b_w_q[i].astype(BF16), _row(b_b_q[i]), tm=512)
            if i == 0:
                kv = kv_new
            o = _swa(q, kv, b_sinks[i].astype(F32), batch=B, seq=S, tq=512)
            w_o, b_o = b_w_o[i], _row(b_b_o[i])
        h = _ffn(h, o, w_o.astype(BF16), b_o, _row(f_norm[layer]), f_w_up[layer].astype(BF16),
                 f_conv_w[layer], _row(f_conv_b[layer]), f_w_down[layer].astype(BF16),
                 _row(final_norm), seq=S, tm=512, final_norm=(layer == depth - 1))
    return h.reshape(B, S, D)
```

```python
import functools

import jax
import jax.numpy as jnp
from jax import lax
from jax.experimental import pallas as pl
from jax.experimental.pallas import tpu as pltpu

F32 = jnp.float32
BF16 = jnp.bfloat16
EPS = 1e-6

LANES = 128
SUBLANES = 8
VMEM_LIMIT = 56 * 1024 * 1024

A_HEADS = 8
A_DK = 128
A_CONV = 4
A_CHUNK = 64

B_HEADS = 16
B_KV_HEADS = 4
B_GROUP = B_HEADS // B_KV_HEADS
B_DH = 64
B_WINDOW = 128

FFN_CONV = 3


def _params(n_axes, vmem=VMEM_LIMIT):
    return pltpu.CompilerParams(dimension_semantics=("arbitrary",) * n_axes,
                                vmem_limit_bytes=vmem)


def _const_spec(shape):
    nd = len(shape)
    return pl.BlockSpec(shape, lambda *_: (0,) * nd, pipeline_mode=pl.Buffered(1))


def _rms_scale(x):
    return lax.rsqrt(jnp.mean(x * x, axis=-1, keepdims=True) + EPS)


def _sigmoid(x):
    return 1.0 / (1.0 + jnp.exp(-x))


def _silu(x):
    return x * _sigmoid(x)


def _softplus(x):
    return jnp.maximum(x, 0.0) + jnp.log(1.0 + jnp.exp(-jnp.abs(x)))


def _in_proj_kernel(x_ref, g_ref, w_ref, ws_ref, cw_ref, q_ref, k_ref, v_ref, gt_ref, ps_ref,
                    xp_scr, *, n_chunk, tiles_per_seq):
    i = pl.program_id(0)
    tm = x_ref.shape[0]
    H, _, DK = q_ref.shape
    per = n_chunk // LANES

    @pl.when(i % tiles_per_seq == 0)
    def _():
        xp_scr[:, 0:SUBLANES, :] = jnp.zeros((xp_scr.shape[0], SUBLANES, LANES), F32)

    x = x_ref[...]
    hn = (x * _rms_scale(x) * g_ref[...]).astype(BF16)
    def project(c):
        res = jnp.dot(hn, w_ref[:, c * n_chunk:(c + 1) * n_chunk], preferred_element_type=F32)
        for j in range(per):
            sec, h = divmod(c * per + j, H)
            u = res[:, j * LANES:(j + 1) * LANES]
            if sec == 3:
                gt_ref[h] = _silu(u).astype(gt_ref.dtype)
            else:
                xp_scr[c * per + j, SUBLANES:SUBLANES + tm, :] = u

    def finish(c):
        for j in range(per):
            slab = c * per + j
            sec, h = divmod(slab, H)
            if sec == 3:
                continue
            u = xp_scr[slab, SUBLANES:SUBLANES + tm, :]
            w = cw_ref[sec, h]
            y = u * w[A_CONV - 1:A_CONV, :]
            for t in range(A_CONV - 1):
                off = SUBLANES - (A_CONV - 1) + t
                y = y + xp_scr[slab, off:off + tm, :] * w[t:t + 1, :]
            xp_scr[slab, 0:SUBLANES, :] = u[tm - SUBLANES:, :]
            y = _silu(y)
            if sec == 0:
                scale = lax.rsqrt(jnp.sum(y * y, axis=-1, keepdims=True) + EPS) * (DK ** -0.5)
                q_ref[h] = (y * scale).astype(q_ref.dtype)
            elif sec == 1:
                k_ref[h] = y * lax.rsqrt(jnp.sum(y * y, axis=-1, keepdims=True) + EPS)
            else:
                v_ref[h] = y

    n_chunks = 4 * H // per
    project(0)
    for c in range(n_chunks):
        if c + 1 < n_chunks:
            project(c + 1)
        finish(c)
    ps_ref[...] = jnp.dot(hn, ws_ref[...], preferred_element_type=F32)


def _in_proj(x2, g, w_main, w_small, cw, *, seq, tm):
    T, D = x2.shape
    N = w_main.shape[1]
    H = A_HEADS
    head_major = pl.BlockSpec((H, tm, A_DK), lambda i: (0, i, 0))
    return pl.pallas_call(
        functools.partial(_in_proj_kernel, n_chunk=512, tiles_per_seq=seq // tm),
        grid=(T // tm,),
        in_specs=[pl.BlockSpec((tm, D), lambda i: (i, 0)),
                  _const_spec((1, D)),
                  _const_spec((D, N)),
                  _const_spec((D, LANES)),
                  _const_spec((3, H, A_CONV, A_DK))],
        out_specs=[head_major, head_major, head_major, head_major,
                   pl.BlockSpec((tm, LANES), lambda i: (i, 0))],
        out_shape=[jax.ShapeDtypeStruct((H, T, A_DK), BF16),
                   jax.ShapeDtypeStruct((H, T, A_DK), F32),
                   jax.ShapeDtypeStruct((H, T, A_DK), F32),
                   jax.ShapeDtypeStruct((H, T, A_DK), BF16),
                   jax.ShapeDtypeStruct((T, LANES), F32)],
        scratch_shapes=[pltpu.VMEM((3 * H, tm + SUBLANES, LANES), F32)],
        compiler_params=_params(1),
        name="in_proj",
    )(x2, g, w_main, w_small, cw)


def _unit_lower_inverses(lmats):
    n = lmats[0].shape[0]
    n_tiles = n // SUBLANES
    lane = lax.broadcasted_iota(jnp.int32, (SUBLANES, n), 1)
    sub = lax.broadcasted_iota(jnp.int32, (SUBLANES, n), 0)
    blk = lane // SUBLANES
    base = blk * SUBLANES
    eye = jnp.where(lane - base == sub, 1.0, 0.0).astype(F32)
    zero = jnp.zeros((SUBLANES, n), F32)
    lts = [[lm[t * SUBLANES:(t + 1) * SUBLANES, :] for t in range(n_tiles)] for lm in lmats]
    tts = []
    for lt in lts:
        ld = lt[n_tiles - 1]
        for t in reversed(range(n_tiles - 1)):
            ld = jnp.where(blk == t, lt[t], ld)
        td = eye
        for m in range(SUBLANES - 1):
            col = jnp.take_along_axis(ld, base + m, axis=1)
            td = td - col * td[m:m + 1, :]
        tts.append([jnp.where(blk == t, td, 0.0) for t in range(n_tiles)])
    s = SUBLANES
    while s < n:
        odd = [t for t in range(n_tiles) if (t * SUBLANES // s) % 2 == 1]
        ws = []
        for lt, tt in zip(lts, tts):
            l_odd = jnp.concatenate(
                [jnp.where(lane // s == t * SUBLANES // s - 1, lt[t], 0.0) for t in odd], axis=0)
            t_full = jnp.concatenate(tt, axis=0).astype(BF16)
            ws.append(jnp.dot(l_odd.astype(BF16), t_full, preferred_element_type=F32))
        xs = []
        for tt, w in zip(tts, ws):
            w_rows = {t: w[i * SUBLANES:(i + 1) * SUBLANES, :] for i, t in enumerate(odd)}
            w_full = jnp.concatenate([w_rows.get(t, zero) for t in range(n_tiles)], axis=0)
            t_odd = jnp.concatenate([tt[t] for t in odd], axis=0).astype(BF16)
            xs.append(jnp.dot(t_odd, w_full.astype(BF16), preferred_element_type=F32))
        for tt, x in zip(tts, xs):
            for i, t in enumerate(odd):
                tt[t] = tt[t] - x[i * SUBLANES:(i + 1) * SUBLANES, :]
        s *= 2
    return [jnp.concatenate(tt, axis=0).astype(BF16) for tt in tts]


def _delta_kernel(q_ref, k_ref, v_ref, gt_ref, ps_ref, arow_ref, brow_ref, on_ref,
                  o_ref,
                  state_scr, kq_scr, kt_scr, gc_scr, beta_scr, grow_scr,
                  attn_scr, rhs_scr, val_scr, egc_scr, dl_scr, gam_scr, o_scr, *, group):
    s = pl.program_id(1)
    H, Tb, DK = q_ref.shape
    C = A_CHUNK
    n_chunks = Tb // C

    @pl.when(s == 0)
    def _():
        state_scr[...] = jnp.zeros_like(state_scr)

    for h in range(H):
        qb = q_ref[h]
        k = k_ref[h]
        kb = k.astype(BF16)
        ktb = k.T.astype(BF16)
        for c in range(n_chunks):
            kq_scr[h, c, 0:C, :] = kb[c * C:(c + 1) * C, :]
            kq_scr[h, c, C:2 * C, :] = qb[c * C:(c + 1) * C, :]
            kt_scr[h, c] = ktb[:, c * C:(c + 1) * C]

    ps = ps_ref[...]
    beta_scr[...] = _sigmoid(ps)
    g = -jnp.exp(arow_ref[...]) * _softplus(ps + brow_ref[...])
    pos = lax.broadcasted_iota(jnp.int32, (Tb, LANES), 0) % C
    gc = g
    sh = 1
    while sh < C:
        gc = gc + jnp.where(pos >= sh, pltpu.roll(gc, sh, 0), 0.0)
        sh *= 2
    gc_scr[...] = gc
    gct = gc.T
    for c in range(n_chunks):
        grow_scr[c] = gct[:, c * C:(c + 1) * C]

    ri = lax.broadcasted_iota(jnp.int32, (C, C), 0)
    ci = lax.broadcasted_iota(jnp.int32, (C, C), 1)
    incl = ri >= ci
    strict = ri > ci

    def gate_cols(c, h):
        gcol = jnp.broadcast_to(gc_scr[c * C:(c + 1) * C, H + h:H + h + 1], (C, DK))
        bcol = jnp.broadcast_to(beta_scr[c * C:(c + 1) * C, h:h + 1], (C, DK))
        return gcol, bcol

    for c0 in range(0, n_chunks, group):
        chains = [(c, h) for c in range(c0, c0 + group) for h in range(H)]
        scs = [jnp.dot(kq_scr[h, c], kt_scr[h, c], preferred_element_type=F32)
               for c, h in chains]
        lmats = []
        for (c, h), sc in zip(chains, scs):
            gcol, bcol = gate_cols(c, h)
            grow = grow_scr[c, H + h:H + h + 1, :]
            decay = jnp.exp(jnp.where(incl, gcol[:, :C] - grow, -jnp.inf))
            lmats.append(jnp.where(strict, sc[:C] * (decay * bcol[:, :C]), 0.0))
            attn_scr[h, c] = (sc[C:] * decay).astype(BF16)
            glast = gcol[C - 1:C, :]
            egc = jnp.exp(gcol)
            k = k_ref[h, c * C:(c + 1) * C, :]
            v = v_ref[h, c * C:(c + 1) * C, :]
            rhs_scr[h, c] = jnp.concatenate([k * (bcol * egc), v * bcol], axis=1).astype(BF16)
            egc_scr[h, c] = egc
            dl_scr[h, c] = jnp.exp(glast - gcol)
            gam_scr[h, c] = jnp.broadcast_to(jnp.exp(glast), (SUBLANES, DK))
        tmats = _unit_lower_inverses(lmats)
        sols = [jnp.dot(tmat, rhs_scr[h, c], preferred_element_type=F32)
                for (c, h), tmat in zip(chains, tmats)]
        for (c, h), sol in zip(chains, sols):
            kq_scr[h, c, 0:C, :] = sol[:, :DK].astype(BF16)
            val_scr[h, c] = sol[:, DK:]

    for c in range(n_chunks):
        sts = [state_scr[h] for h in range(H)]
        prs = [jnp.dot(kq_scr[h, c], sts[h].astype(BF16), preferred_element_type=F32)
               for h in range(H)]
        v_news = [val_scr[h, c] - prs[h][:C] for h in range(H)]
        ats = [jnp.dot(attn_scr[h, c], v_news[h].astype(BF16), preferred_element_type=F32)
               for h in range(H)]
        kvs = [jnp.dot(kt_scr[h, c], (dl_scr[h, c] * v_news[h]).astype(BF16),
                       preferred_element_type=F32) for h in range(H)]
        for h in range(H):
            state_scr[h] = sts[h] * gam_scr[h, c][0:1, :] + kvs[h]
            o_scr[h, c * C:(c + 1) * C, :] = egc_scr[h, c] * prs[h][C:] + ats[h]

    for h in range(H):
        o = o_scr[h]
        gate = gt_ref[h]
        y = o * _rms_scale(o) * on_ref[...] * gate
        o_ref[:, h * DK:(h + 1) * DK] = y.astype(o_ref.dtype)


def _delta(q, k, v, gate, ps, arow, brow, onorm, *, batch, seq, tb):
    H = A_HEADS
    DK = A_DK
    T = batch * seq
    ns = seq // tb
    n_chunks = tb // A_CHUNK
    head_major = pl.BlockSpec((H, tb, DK), lambda b, s: (0, b * ns + s, 0))

    return pl.pallas_call(
        functools.partial(_delta_kernel, group=n_chunks),
        grid=(batch, ns),
        in_specs=[head_major, head_major, head_major, head_major,
                  pl.BlockSpec((tb, LANES), lambda b, s: (b * ns + s, 0)),
                  _const_spec((1, LANES)), _const_spec((1, LANES)), _const_spec((1, DK))],
        out_specs=pl.BlockSpec((tb, H * DK), lambda b, s: (b * ns + s, 0)),
        out_shape=jax.ShapeDtypeStruct((T, H * DK), BF16),
        scratch_shapes=[
            pltpu.VMEM((H, DK, DK), F32),
            pltpu.VMEM((H, n_chunks, 2 * A_CHUNK, DK), BF16),
            pltpu.VMEM((H, n_chunks, DK, A_CHUNK), BF16),
            pltpu.VMEM((tb, LANES), F32),
            pltpu.VMEM((tb, LANES), F32),
            pltpu.VMEM((n_chunks, LANES, A_CHUNK), F32),
            pltpu.VMEM((H, n_chunks, A_CHUNK, A_CHUNK), BF16),
            pltpu.VMEM((H, n_chunks, A_CHUNK, 2 * DK), BF16),
            pltpu.VMEM((H, n_chunks, A_CHUNK, DK), F32),
            pltpu.VMEM((H, n_chunks, A_CHUNK, DK), F32),
            pltpu.VMEM((H, n_chunks, A_CHUNK, DK), F32),
            pltpu.VMEM((H, n_chunks, SUBLANES, DK), F32),
            pltpu.VMEM((H, tb, DK), F32),
        ],
        compiler_params=_params(2),
        name="delta",
    )(q, k, v, gate, ps, arow, brow, onorm)


def _ffn_kernel(x_ref, o_ref, wo_ref, bo_ref, g_ref, wu_ref, cw_ref, cb_ref, wd_ref, fg_ref,
                *rest, tiles_per_seq, n_chunk, final_norm, with_qkv):
    if with_qkv:
        (gkv_ref, wkv_ref, bkv_ref, gq_ref, wq_ref, bq_ref,
         out_ref, kv_ref, q_ref, halo_scr, up_scr, act_scr) = rest
    else:
        out_ref, halo_scr, up_scr, act_scr = rest
    i = pl.program_id(0)
    tm = x_ref.shape[0]
    F = wd_ref.shape[0]

    @pl.when(i % tiles_per_seq == 0)
    def _():
        halo_scr[...] = jnp.zeros_like(halo_scr)

    x = (x_ref[...] + bo_ref[...]
         + jnp.dot(o_ref[...], wo_ref[...], preferred_element_type=F32))
    hn = (x * _rms_scale(x) * g_ref[...]).astype(BF16)

    def conv_cols(c0):
        u = jnp.dot(hn, wu_ref[:, c0:c0 + n_chunk], preferred_element_type=F32)
        up_scr[0:SUBLANES, :] = halo_scr[:, c0:c0 + n_chunk]
        up_scr[SUBLANES:SUBLANES + tm, :] = u
        halo_scr[:, c0:c0 + n_chunk] = u[tm - SUBLANES:, :]
        w = cw_ref[:, c0:c0 + n_chunk]
        y = u * w[FFN_CONV - 1:FFN_CONV, :] + cb_ref[:, c0:c0 + n_chunk]
        for j in range(FFN_CONV - 1):
            off = SUBLANES - (FFN_CONV - 1) + j
            y = y + up_scr[off:off + tm, :] * w[j:j + 1, :]
        return y

    for c in range(F // n_chunk):
        gate = conv_cols(c * n_chunk)
        up = conv_cols(F + c * n_chunk)
        act_scr[:, c * n_chunk:(c + 1) * n_chunk] = (_silu(gate) * up).astype(BF16)

    out = x + jnp.dot(act_scr[...], wd_ref[...], preferred_element_type=F32)
    if with_qkv:
        xn = out * _rms_scale(out)
        kv = jnp.dot((xn * gkv_ref[...]).astype(BF16), wkv_ref[...], preferred_element_type=F32)
        kv_ref[...] = (kv + bkv_ref[...]).astype(kv_ref.dtype)
        q = jnp.dot((xn * gq_ref[...]).astype(BF16), wq_ref[...], preferred_element_type=F32)
        q_ref[...] = ((q + bq_ref[...]) * (B_DH ** -0.5)).astype(q_ref.dtype)
    if final_norm:
        out = out * _rms_scale(out) * fg_ref[...]
    out_ref[...] = out


def _ffn(h2, o, w_o, b_o, g, w_up, conv_w, conv_b, w_down, final_g, qkv, *, seq, tm, final_norm):
    T, D = h2.shape
    F = w_down.shape[0]
    KO = o.shape[1]
    n_chunk = 256
    row_spec = pl.BlockSpec((tm, D), lambda i: (i, 0))
    in_specs = [row_spec,
                pl.BlockSpec((tm, KO), lambda i: (i, 0)),
                _const_spec((KO, D)),
                _const_spec((1, D)),
                _const_spec((1, D)),
                _const_spec((D, 2 * F)),
                _const_spec((FFN_CONV, 2 * F)),
                _const_spec((1, 2 * F)),
                _const_spec((F, D)),
                _const_spec((1, D))]
    out_specs = [row_spec]
    out_shape = [jax.ShapeDtypeStruct((T, D), F32)]
    args = [h2, o, w_o, b_o, g, w_up, conv_w, conv_b, w_down, final_g]
    if qkv is not None:
        nkv, nq = qkv[1].shape[1], qkv[4].shape[1]
        in_specs += [_const_spec(a.shape) for a in qkv]
        out_specs += [pl.BlockSpec((tm, nkv), lambda i: (i, 0)),
                      pl.BlockSpec((tm, nq), lambda i: (i, 0))]
        out_shape += [jax.ShapeDtypeStruct((T, nkv), BF16), jax.ShapeDtypeStruct((T, nq), BF16)]
        args += list(qkv)
    return pl.pallas_call(
        functools.partial(_ffn_kernel, tiles_per_seq=seq // tm, n_chunk=n_chunk,
                          final_norm=final_norm, with_qkv=qkv is not None),
        grid=(T // tm,),
        in_specs=in_specs,
        out_specs=out_specs,
        out_shape=out_shape,
        scratch_shapes=[
            pltpu.VMEM((SUBLANES, 2 * F), F32),
            pltpu.VMEM((tm + SUBLANES, n_chunk), F32),
            pltpu.VMEM((tm, F), BF16),
        ],
        compiler_params=_params(1),
        name="ffn",
    )(*args)


def _swa_kernel(sink_ref, q_ref, kvp_ref, kvc_ref, o_ref):
    j = pl.program_id(1)
    W = B_WINDOW
    TQ = q_ref.shape[0]
    nkv = B_KV_HEADS * B_DH
    per_tile = LANES // B_DH
    ri = lax.broadcasted_iota(jnp.int32, (W, W), 0)
    ci = lax.broadcasted_iota(jnp.int32, (W, W), 1)
    upper = ci > ri
    dist = jnp.where(upper, ri - ci + W, ri - ci).astype(F32)
    no_prev = jnp.logical_and(upper, j == 0)
    assert per_tile == 2
    nt = (((1,), (1,)), ((), ()))

    def window(r0, col):
        cur = kvc_ref[r0:r0 + W, col:col + LANES]
        prev = kvp_ref[:, col:col + LANES] if r0 == 0 else kvc_ref[r0 - W:r0, col:col + LANES]
        return jnp.concatenate([prev, cur], axis=0)

    lo_kv = lax.broadcasted_iota(jnp.int32, (2 * W, LANES), 1) < B_DH
    lo_q = lax.broadcasted_iota(jnp.int32, (W, LANES), 1) < B_DH

    def both_halves(x, half):
        swapped = jnp.concatenate([x[:, B_DH:], x[:, :B_DH]], axis=1)
        return jnp.where(lo_kv, x, swapped) if half == 0 else jnp.where(lo_kv, swapped, x)

    def keep_half(x, lo_mask, e):
        return jnp.where(lo_mask, x, jnp.zeros_like(x)) if e == 0 else jnp.where(lo_mask, jnp.zeros_like(x), x)

    def scores(unit):
        sb, kh = unit
        r0 = sb * W
        tile, half = divmod(kh, per_tile)
        kd = both_halves(window(r0, tile * LANES), half)
        out = []
        for pair in range(B_GROUP // per_tile):
            h0 = kh * B_GROUP + pair * per_tile
            qp = q_ref[r0:r0 + W, h0 * B_DH:h0 * B_DH + LANES]
            out.append([lax.dot_general(keep_half(qp, lo_q, e), kd, nt, preferred_element_type=F32)
                        for e in range(per_tile)])
        return out

    def finish(unit, s2s):
        sb, kh = unit
        r0 = sb * W
        tile, half = divmod(kh, per_tile)
        vd = both_halves(window(r0, nkv + tile * LANES), half)
        v_pad = [keep_half(vd, lo_kv, e) for e in range(per_tile)]
        p2s = []
        for pair in range(B_GROUP // per_tile):
            h0 = kh * B_GROUP + pair * per_tile
            for e in range(per_tile):
                head = h0 + e
                slope = 2.0 ** (-8.0 * (head + 1) / B_HEADS)
                s2 = s2s[pair][e]
                s = jnp.where(upper, s2[:, :W], s2[:, W:]) - slope * dist
                if sb == 0:
                    s = jnp.where(no_prev, -jnp.inf, s)
                sink = sink_ref[head]
                m = jnp.maximum(jnp.max(s, axis=-1, keepdims=True), sink)
                p = jnp.exp(s - m)
                denom = jnp.sum(p, axis=-1, keepdims=True) + jnp.exp(sink - m)
                p = p * (1.0 / denom)
                p2s.append(jnp.concatenate([jnp.where(upper, p, 0.0), jnp.where(upper, 0.0, p)],
                                           axis=1).astype(BF16))
        for pair in range(B_GROUP // per_tile):
            h0 = kh * B_GROUP + pair * per_tile
            acc = None
            for e in range(per_tile):
                part = jnp.dot(p2s[pair * per_tile + e], v_pad[e],
                               preferred_element_type=F32)
                acc = part if acc is None else acc + part
            o_ref[r0:r0 + W, h0 * B_DH:h0 * B_DH + LANES] = acc.astype(o_ref.dtype)

    units = [(sb, kh) for sb in range(TQ // W) for kh in range(B_KV_HEADS)]
    ahead = 1
    pending = [scores(unit) for unit in units[:ahead]]
    for u, unit in enumerate(units):
        if u + ahead < len(units):
            pending.append(scores(units[u + ahead]))
        finish(unit, pending.pop(0))


def _swa(q, kv, sinks, *, batch, seq, tq):
    T, NQ = q.shape
    NKV2 = kv.shape[1]
    W = B_WINDOW
    nq = seq // tq
    r = tq // W
    return pl.pallas_call(
        _swa_kernel,
        grid=(batch, nq),
        in_specs=[pl.BlockSpec(memory_space=pltpu.SMEM),
                  pl.BlockSpec((tq, NQ), lambda b, j: (b * nq + j, 0)),
                  pl.BlockSpec((W, NKV2), lambda b, j: (jnp.maximum((b * nq + j) * r - 1, 0), 0)),
                  pl.BlockSpec((tq, NKV2), lambda b, j: (b * nq + j, 0))],
        out_specs=pl.BlockSpec((tq, NQ), lambda b, j: (b * nq + j, 0)),
        out_shape=jax.ShapeDtypeStruct((T, NQ), BF16),
        compiler_params=_params(2),
        name="swa",
    )(sinks, q, kv, kv)


def _row(v):
    return v.reshape(1, -1).astype(F32)


def _delta_mixer(h2, norm_g, w_in, conv_w, a_log, dt_bias, onorm, *, batch, seq):
    H = A_HEADS
    n_main = 4 * H * A_DK
    w_main = w_in[:, :n_main].astype(BF16)
    w_small = jnp.pad(w_in[:, n_main:], ((0, 0), (0, LANES - 2 * H))).astype(BF16)
    cw = conv_w.reshape(A_CONV, 3, H, A_DK).transpose(1, 2, 0, 3)
    q, k, v, gate, ps = _in_proj(h2, _row(norm_g), w_main, w_small, cw, seq=seq, tm=256)
    arow = jnp.pad(a_log, (H, LANES - 2 * H)).reshape(1, LANES)
    brow = jnp.pad(dt_bias, (H, LANES - 2 * H)).reshape(1, LANES)
    return _delta(q, k, v, gate, ps, arow, brow, _row(onorm), batch=batch, seq=seq, tb=512)


def kernel(x, a_norm, a_w_in, a_conv_w, a_A_log, a_dt_bias, a_onorm, a_w_out, kv_norm, kv_w, kv_b, b_norm, b_w_q, b_b_q, b_sinks, b_w_o, b_b_o, f_norm, f_w_up, f_conv_w, f_conv_b, f_w_down, final_norm):
    B, S, D = x.shape
    depth = f_norm.shape[0]
    n_a = a_norm.shape[0]
    assert n_a >= 1, "the first attention layer takes its kv / q from the preceding ffn call"
    h = x.reshape(B * S, D)
    kv = q = None
    for layer in range(depth):
        if layer < n_a:
            i = layer
            o = _delta_mixer(h, a_norm[i], a_w_in[i], a_conv_w[i], a_A_log[i], a_dt_bias[i],
                             a_onorm[i], batch=B, seq=S)
            w_o, b_o = a_w_out[i], jnp.zeros((1, D), F32)
        else:
            i = layer - n_a
            o = _swa(q, kv, b_sinks[i].astype(F32), batch=B, seq=S, tq=512)
            w_o, b_o = b_w_o[i], _row(b_b_o[i])
        qkv = None
        if n_a <= layer + 1 < depth:
            j = layer + 1 - n_a
            qkv = (_row(kv_norm), kv_w.astype(BF16), _row(kv_b),
                   _row(b_norm[j]), b_w_q[j].astype(BF16), _row(b_b_q[j]))
        res = _ffn(h, o, w_o.astype(BF16), b_o, _row(f_norm[layer]), f_w_up[layer].astype(BF16),
                   f_conv_w[layer], _row(f_conv_b[layer]), f_w_down[layer].astype(BF16),
                   _row(final_norm), qkv, seq=S, tm=512, final_norm=(layer == depth - 1))
        h = res[0]
        if qkv is not None:
            q = res[2]
            if layer + 1 == n_a:
                kv = res[1]
    return h.reshape(B, S, D)
```

```python
import functools

import jax
import jax.numpy as jnp
from jax import lax
from jax.experimental import pallas as pl
from jax.experimental.pallas import tpu as pltpu

F32 = jnp.float32
BF16 = jnp.bfloat16
EPS = 1e-6

LANES = 128
SUBLANES = 8
VMEM_LIMIT = 56 * 1024 * 1024

A_HEADS = 8
A_DK = 128
A_CONV = 4
A_CHUNK = 64

B_HEADS = 16
B_KV_HEADS = 4
B_GROUP = B_HEADS // B_KV_HEADS
B_DH = 64
B_WINDOW = 128

FFN_CONV = 3


def _params(n_axes, vmem=VMEM_LIMIT):
    return pltpu.CompilerParams(dimension_semantics=("arbitrary",) * n_axes,
                                vmem_limit_bytes=vmem)


def _const_spec(shape):
    nd = len(shape)
    return pl.BlockSpec(shape, lambda *_: (0,) * nd, pipeline_mode=pl.Buffered(1))


def _layer_spec(stacked, layer):
    shape = stacked.shape[1:]
    nd = len(shape)
    return pl.BlockSpec((None,) + shape, lambda *_: (layer,) + (0,) * nd,
                        pipeline_mode=pl.Buffered(1))


def _rms_scale(x):
    return lax.rsqrt(jnp.mean(x * x, axis=-1, keepdims=True) + EPS)


def _sigmoid(x):
    return 1.0 / (1.0 + jnp.exp(-x))


def _silu(x):
    return x * _sigmoid(x)


def _softplus(x):
    return jnp.maximum(x, 0.0) + jnp.log(1.0 + jnp.exp(-jnp.abs(x)))


def _in_proj_kernel(x_ref, g_ref, w_ref, ws_ref, cw_ref, q_ref, k_ref, v_ref, gt_ref, ps_ref,
                    xp_scr, *, n_chunk, tiles_per_seq):
    i = pl.program_id(0)
    tm = x_ref.shape[0]
    H, _, DK = q_ref.shape
    per = n_chunk // LANES

    @pl.when(i % tiles_per_seq == 0)
    def _():
        xp_scr[:, 0:SUBLANES, :] = jnp.zeros((xp_scr.shape[0], SUBLANES, LANES), F32)

    x = x_ref[...]
    hn = (x * _rms_scale(x) * g_ref[...]).astype(BF16)
    def project(c):
        res = jnp.dot(hn, w_ref[:, c * n_chunk:(c + 1) * n_chunk], preferred_element_type=F32)
        for j in range(per):
            sec, h = divmod(c * per + j, H)
            u = res[:, j * LANES:(j + 1) * LANES]
            if sec == 3:
                gt_ref[h] = _silu(u).astype(gt_ref.dtype)
            else:
                xp_scr[c * per + j, SUBLANES:SUBLANES + tm, :] = u

    def finish(c):
        for j in range(per):
            slab = c * per + j
            sec, h = divmod(slab, H)
            if sec == 3:
                continue
            u = xp_scr[slab, SUBLANES:SUBLANES + tm, :]
            w = cw_ref[sec, h]
            y = u * w[A_CONV - 1:A_CONV, :]
            for t in range(A_CONV - 1):
                off = SUBLANES - (A_CONV - 1) + t
                y = y + xp_scr[slab, off:off + tm, :] * w[t:t + 1, :]
            xp_scr[slab, 0:SUBLANES, :] = u[tm - SUBLANES:, :]
            y = _silu(y)
            if sec == 0:
                scale = lax.rsqrt(jnp.sum(y * y, axis=-1, keepdims=True) + EPS) * (DK ** -0.5)
                q_ref[h] = (y * scale).astype(q_ref.dtype)
            elif sec == 1:
                scale = lax.rsqrt(jnp.sum(y * y, axis=-1, keepdims=True) + EPS)
                k_ref[h] = (y * scale).astype(k_ref.dtype)
            else:
                v_ref[h] = y.astype(v_ref.dtype)

    n_chunks = 4 * H // per
    project(0)
    for c in range(n_chunks):
        if c + 1 < n_chunks:
            project(c + 1)
        finish(c)
    ps_ref[...] = jnp.dot(hn, ws_ref[...], preferred_element_type=F32)


def _in_proj(x2, g, w_all, layer, w_small, cw, *, seq, tm):
    T, D = x2.shape
    H = A_HEADS
    head_major = pl.BlockSpec((H, tm, A_DK), lambda i: (0, i, 0))
    return pl.pallas_call(
        functools.partial(_in_proj_kernel, n_chunk=512, tiles_per_seq=seq // tm),
        grid=(T // tm,),
        in_specs=[pl.BlockSpec((tm, D), lambda i: (i, 0)),
                  _const_spec((1, D)),
                  _layer_spec(w_all, layer),
                  _const_spec((D, LANES)),
                  _const_spec((3, H, A_CONV, A_DK))],
        out_specs=[head_major, head_major, head_major, head_major,
                   pl.BlockSpec((tm, LANES), lambda i: (i, 0))],
        out_shape=[jax.ShapeDtypeStruct((H, T, A_DK), BF16),
                   jax.ShapeDtypeStruct((H, T, A_DK), BF16),
                   jax.ShapeDtypeStruct((H, T, A_DK), BF16),
                   jax.ShapeDtypeStruct((H, T, A_DK), BF16),
                   jax.ShapeDtypeStruct((T, LANES), F32)],
        scratch_shapes=[pltpu.VMEM((3 * H, tm + SUBLANES, LANES), F32)],
        compiler_params=_params(1),
        name="in_proj",
    )(x2, g, w_all, w_small, cw)


def _unit_lower_inverses(lmats):
    n = lmats[0].shape[0]
    n_tiles = n // SUBLANES
    lane = lax.broadcasted_iota(jnp.int32, (SUBLANES, n), 1)
    sub = lax.broadcasted_iota(jnp.int32, (SUBLANES, n), 0)
    blk = lane // SUBLANES
    base = blk * SUBLANES
    eye = jnp.where(lane - base == sub, 1.0, 0.0).astype(F32)
    zero = jnp.zeros((SUBLANES, n), F32)
    lts = [[lm[t * SUBLANES:(t + 1) * SUBLANES, :] for t in range(n_tiles)] for lm in lmats]
    tts = []
    for lt in lts:
        ld = lt[n_tiles - 1]
        for t in reversed(range(n_tiles - 1)):
            ld = jnp.where(blk == t, lt[t], ld)
        td = eye
        for m in range(SUBLANES - 1):
            col = jnp.take_along_axis(ld, base + m, axis=1)
            td = td - col * td[m:m + 1, :]
        tts.append([jnp.where(blk == t, td, 0.0) for t in range(n_tiles)])
    s = SUBLANES
    while s < n:
        odd = [t for t in range(n_tiles) if (t * SUBLANES // s) % 2 == 1]
        ws = []
        for lt, tt in zip(lts, tts):
            l_odd = jnp.concatenate(
                [jnp.where(lane // s == t * SUBLANES // s - 1, lt[t], 0.0) for t in odd], axis=0)
            t_full = jnp.concatenate(tt, axis=0).astype(BF16)
            ws.append(jnp.dot(l_odd.astype(BF16), t_full, preferred_element_type=F32))
        xs = []
        for tt, w in zip(tts, ws):
            w_rows = {t: w[i * SUBLANES:(i + 1) * SUBLANES, :] for i, t in enumerate(odd)}
            w_full = jnp.concatenate([w_rows.get(t, zero) for t in range(n_tiles)], axis=0)
            t_odd = jnp.concatenate([tt[t] for t in odd], axis=0).astype(BF16)
            xs.append(jnp.dot(t_odd, w_full.astype(BF16), preferred_element_type=F32))
        for tt, x in zip(tts, xs):
            for i, t in enumerate(odd):
                tt[t] = tt[t] - x[i * SUBLANES:(i + 1) * SUBLANES, :]
        s *= 2
    return [jnp.concatenate(tt, axis=0).astype(BF16) for tt in tts]


def _delta_kernel(q_ref, k_ref, v_ref, gt_ref, ps_ref, arow_ref, brow_ref, on_ref,
                  o_ref,
                  state_scr, kq_scr, kt_scr, gc_scr, beta_scr, grow_scr,
                  attn_scr, rhs_scr, val_scr, egc_scr, dl_scr, gam_scr, o_scr, *, group):
    s = pl.program_id(1)
    H, Tb, DK = q_ref.shape
    C = A_CHUNK
    n_chunks = Tb // C

    @pl.when(s == 0)
    def _():
        state_scr[...] = jnp.zeros_like(state_scr)

    for h in range(H):
        qb = q_ref[h]
        kb = k_ref[h]
        ktb = kb.astype(F32).T.astype(BF16)
        for c in range(n_chunks):
            kq_scr[h, c, 0:C, :] = kb[c * C:(c + 1) * C, :]
            kq_scr[h, c, C:2 * C, :] = qb[c * C:(c + 1) * C, :]
            kt_scr[h, c] = ktb[:, c * C:(c + 1) * C]

    ps = ps_ref[...]
    beta_scr[...] = _sigmoid(ps)
    g = -jnp.exp(arow_ref[...]) * _softplus(ps + brow_ref[...])
    pos = lax.broadcasted_iota(jnp.int32, (Tb, LANES), 0) % C
    gc = g
    sh = 1
    while sh < C:
        gc = gc + jnp.where(pos >= sh, pltpu.roll(gc, sh, 0), 0.0)
        sh *= 2
    gc_scr[...] = gc
    gct = gc.T
    for c in range(n_chunks):
        grow_scr[c] = gct[:, c * C:(c + 1) * C]

    ri = lax.broadcasted_iota(jnp.int32, (C, C), 0)
    ci = lax.broadcasted_iota(jnp.int32, (C, C), 1)
    incl = ri >= ci
    strict = ri > ci

    def gate_cols(c, h):
        gcol = jnp.broadcast_to(gc_scr[c * C:(c + 1) * C, H + h:H + h + 1], (C, DK))
        bcol = jnp.broadcast_to(beta_scr[c * C:(c + 1) * C, h:h + 1], (C, DK))
        return gcol, bcol

    for c0 in range(0, n_chunks, group):
        chains = [(c, h) for c in range(c0, c0 + group) for h in range(H)]
        scs = [jnp.dot(kq_scr[h, c], kt_scr[h, c], preferred_element_type=F32)
               for c, h in chains]
        lmats = []
        for (c, h), sc in zip(chains, scs):
            gcol, bcol = gate_cols(c, h)
            grow = grow_scr[c, H + h:H + h + 1, :]
            decay = jnp.exp(jnp.where(incl, gcol[:, :C] - grow, -jnp.inf))
            lmats.append(jnp.where(strict, sc[:C] * (decay * bcol[:, :C]), 0.0))
            attn_scr[h, c] = (sc[C:] * decay).astype(BF16)
            glast = gcol[C - 1:C, :]
            egc = jnp.exp(gcol)
            k = k_ref[h, c * C:(c + 1) * C, :].astype(F32)
            v = v_ref[h, c * C:(c + 1) * C, :].astype(F32)
            rhs_scr[h, c] = jnp.concatenate([k * (bcol * egc), v * bcol], axis=1).astype(BF16)
            egc_scr[h, c] = egc
            dl_scr[h, c] = jnp.exp(glast - gcol)
            gam_scr[h, c] = jnp.broadcast_to(jnp.exp(glast), (SUBLANES, DK))
        tmats = _unit_lower_inverses(lmats)
        sols = [jnp.dot(tmat, rhs_scr[h, c], preferred_element_type=F32)
                for (c, h), tmat in zip(chains, tmats)]
        for (c, h), sol in zip(chains, sols):
            kq_scr[h, c, 0:C, :] = sol[:, :DK].astype(BF16)
            val_scr[h, c] = sol[:, DK:]

    for c in range(n_chunks):
        sts = [state_scr[h] for h in range(H)]
        prs = [jnp.dot(kq_scr[h, c], sts[h].astype(BF16), preferred_element_type=F32)
               for h in range(H)]
        v_news = [val_scr[h, c] - prs[h][:C] for h in range(H)]
        ats = [jnp.dot(attn_scr[h, c], v_news[h].astype(BF16), preferred_element_type=F32)
               for h in range(H)]
        kvs = [jnp.dot(kt_scr[h, c], (dl_scr[h, c] * v_news[h]).astype(BF16),
                       preferred_element_type=F32) for h in range(H)]
        for h in range(H):
            state_scr[h] = sts[h] * gam_scr[h, c][0:1, :] + kvs[h]
            o_scr[h, c * C:(c + 1) * C, :] = egc_scr[h, c] * prs[h][C:] + ats[h]

    for h in range(H):
        o = o_scr[h]
        gate = gt_ref[h]
        y = o * _rms_scale(o) * on_ref[...] * gate
        o_ref[:, h * DK:(h + 1) * DK] = y.astype(o_ref.dtype)


def _delta(q, k, v, gate, ps, arow, brow, onorm, *, batch, seq, tb):
    H = A_HEADS
    DK = A_DK
    T = batch * seq
    ns = seq // tb
    n_chunks = tb // A_CHUNK
    head_major = pl.BlockSpec((H, tb, DK), lambda b, s: (0, b * ns + s, 0))

    return pl.pallas_call(
        functools.partial(_delta_kernel, group=n_chunks),
        grid=(batch, ns),
        in_specs=[head_major, head_major, head_major, head_major,
                  pl.BlockSpec((tb, LANES), lambda b, s: (b * ns + s, 0)),
                  _const_spec((1, LANES)), _const_spec((1, LANES)), _const_spec((1, DK))],
        out_specs=pl.BlockSpec((tb, H * DK), lambda b, s: (b * ns + s, 0)),
        out_shape=jax.ShapeDtypeStruct((T, H * DK), BF16),
        scratch_shapes=[
            pltpu.VMEM((H, DK, DK), F32),
            pltpu.VMEM((H, n_chunks, 2 * A_CHUNK, DK), BF16),
            pltpu.VMEM((H, n_chunks, DK, A_CHUNK), BF16),
            pltpu.VMEM((tb, LANES), F32),
            pltpu.VMEM((tb, LANES), F32),
            pltpu.VMEM((n_chunks, LANES, A_CHUNK), F32),
            pltpu.VMEM((H, n_chunks, A_CHUNK, A_CHUNK), BF16),
            pltpu.VMEM((H, n_chunks, A_CHUNK, 2 * DK), BF16),
            pltpu.VMEM((H, n_chunks, A_CHUNK, DK), F32),
            pltpu.VMEM((H, n_chunks, A_CHUNK, DK), F32),
            pltpu.VMEM((H, n_chunks, A_CHUNK, DK), F32),
            pltpu.VMEM((H, n_chunks, SUBLANES, DK), F32),
            pltpu.VMEM((H, tb, DK), F32),
        ],
        compiler_params=_params(2),
        name="delta",
    )(q, k, v, gate, ps, arow, brow, onorm)


def _ffn_kernel(x_ref, o_ref, wo_ref, bo_ref, g_ref, wu_ref, cw_ref, cb_ref, wd_ref, fg_ref,
                *rest, tiles_per_seq, n_chunk, final_norm, with_qkv):
    if with_qkv:
        (gkv_ref, wkv_ref, bkv_ref, gq_ref, wq_ref, bq_ref,
         out_ref, kv_ref, q_ref, halo_scr, up_scr, act_scr) = rest
    else:
        out_ref, halo_scr, up_scr, act_scr = rest
    i = pl.program_id(0)
    tm = x_ref.shape[0]
    F = wd_ref.shape[0]

    @pl.when(i % tiles_per_seq == 0)
    def _():
        halo_scr[...] = jnp.zeros_like(halo_scr)

    x = (x_ref[...] + bo_ref[...]
         + jnp.dot(o_ref[...], wo_ref[...], preferred_element_type=F32))
    hn = (x * _rms_scale(x) * g_ref[...]).astype(BF16)

    def conv_cols(c0):
        u = jnp.dot(hn, wu_ref[:, c0:c0 + n_chunk], preferred_element_type=F32)
        up_scr[0:SUBLANES, :] = halo_scr[:, c0:c0 + n_chunk]
        up_scr[SUBLANES:SUBLANES + tm, :] = u
        halo_scr[:, c0:c0 + n_chunk] = u[tm - SUBLANES:, :]
        w = cw_ref[:, c0:c0 + n_chunk]
        y = u * w[FFN_CONV - 1:FFN_CONV, :] + cb_ref[:, c0:c0 + n_chunk]
        for j in range(FFN_CONV - 1):
            off = SUBLANES - (FFN_CONV - 1) + j
            y = y + up_scr[off:off + tm, :] * w[j:j + 1, :]
        return y

    for c in range(F // n_chunk):
        gate = conv_cols(c * n_chunk)
        up = conv_cols(F + c * n_chunk)
        act_scr[:, c * n_chunk:(c + 1) * n_chunk] = (_silu(gate) * up).astype(BF16)

    out = x + jnp.dot(act_scr[...], wd_ref[...], preferred_element_type=F32)
    if with_qkv:
        xn = out * _rms_scale(out)
        kv = jnp.dot((xn * gkv_ref[...]).astype(BF16), wkv_ref[...], preferred_element_type=F32)
        kv_ref[...] = (kv + bkv_ref[...]).astype(kv_ref.dtype)
        q = jnp.dot((xn * gq_ref[...]).astype(BF16), wq_ref[...], preferred_element_type=F32)
        q_ref[...] = ((q + bq_ref[...]) * (B_DH ** -0.5)).astype(q_ref.dtype)
    if final_norm:
        out = out * _rms_scale(out) * fg_ref[...]
    out_ref[...] = out


def _ffn(h2, o, w_o, b_o, g, w_up, conv_w, conv_b, w_down, final_g, qkv, *, layer, seq, tm,
         final_norm):
    T, D = h2.shape
    F = w_down.shape[1]
    KO = o.shape[1]
    n_chunk = 256
    row_spec = pl.BlockSpec((tm, D), lambda i: (i, 0))
    in_specs = [row_spec,
                pl.BlockSpec((tm, KO), lambda i: (i, 0)),
                _const_spec((KO, D)),
                _const_spec((1, D)),
                _const_spec((1, D)),
                _layer_spec(w_up, layer),
                _const_spec((FFN_CONV, 2 * F)),
                _const_spec((1, 2 * F)),
                _layer_spec(w_down, layer),
                _const_spec((1, D))]
    out_specs = [row_spec]
    out_shape = [jax.ShapeDtypeStruct((T, D), F32)]
    args = [h2, o, w_o, b_o, g, w_up, conv_w, conv_b, w_down, final_g]
    if qkv is not None:
        nkv, nq = qkv[1].shape[1], qkv[4].shape[1]
        in_specs += [_const_spec(a.shape) for a in qkv]
        out_specs += [pl.BlockSpec((tm, nkv), lambda i: (i, 0)),
                      pl.BlockSpec((tm, nq), lambda i: (i, 0))]
        out_shape += [jax.ShapeDtypeStruct((T, nkv), BF16), jax.ShapeDtypeStruct((T, nq), BF16)]
        args += list(qkv)
    return pl.pallas_call(
        functools.partial(_ffn_kernel, tiles_per_seq=seq // tm, n_chunk=n_chunk,
                          final_norm=final_norm, with_qkv=qkv is not None),
        grid=(T // tm,),
        in_specs=in_specs,
        out_specs=out_specs,
        out_shape=out_shape,
        scratch_shapes=[
            pltpu.VMEM((SUBLANES, 2 * F), F32),
            pltpu.VMEM((tm + SUBLANES, n_chunk), F32),
            pltpu.VMEM((tm, F), BF16),
        ],
        compiler_params=_params(1),
        name="ffn",
    )(*args)


def _swa_kernel(sink_ref, q_ref, kvp_ref, kvc_ref, o_ref):
    j = pl.program_id(1)
    W = B_WINDOW
    TQ = q_ref.shape[0]
    nkv = B_KV_HEADS * B_DH
    per_tile = LANES // B_DH
    ri = lax.broadcasted_iota(jnp.int32, (W, W), 0)
    ci = lax.broadcasted_iota(jnp.int32, (W, W), 1)
    upper = ci > ri
    dist = jnp.where(upper, ri - ci + W, ri - ci).astype(F32)
    no_prev = jnp.logical_and(upper, j == 0)
    assert per_tile == 2
    nt = (((1,), (1,)), ((), ()))

    def window(r0, col):
        cur = kvc_ref[r0:r0 + W, col:col + LANES]
        prev = kvp_ref[:, col:col + LANES] if r0 == 0 else kvc_ref[r0 - W:r0, col:col + LANES]
        return jnp.concatenate([prev, cur], axis=0)

    lo_kv = lax.broadcasted_iota(jnp.int32, (2 * W, LANES), 1) < B_DH
    lo_q = lax.broadcasted_iota(jnp.int32, (W, LANES), 1) < B_DH

    def both_halves(x, half):
        swapped = jnp.concatenate([x[:, B_DH:], x[:, :B_DH]], axis=1)
        return jnp.where(lo_kv, x, swapped) if half == 0 else jnp.where(lo_kv, swapped, x)

    def keep_half(x, lo_mask, e):
        return jnp.where(lo_mask, x, jnp.zeros_like(x)) if e == 0 else jnp.where(lo_mask, jnp.zeros_like(x), x)

    def scores(unit):
        sb, kh = unit
        r0 = sb * W
        tile, half = divmod(kh, per_tile)
        kd = both_halves(window(r0, tile * LANES), half)
        out = []
        for pair in range(B_GROUP // per_tile):
            h0 = kh * B_GROUP + pair * per_tile
            qp = q_ref[r0:r0 + W, h0 * B_DH:h0 * B_DH + LANES]
            out.append([lax.dot_general(keep_half(qp, lo_q, e), kd, nt, preferred_element_type=F32)
                        for e in range(per_tile)])
        return out

    def finish(unit, s2s):
        sb, kh = unit
        r0 = sb * W
        tile, half = divmod(kh, per_tile)
        vd = both_halves(window(r0, nkv + tile * LANES), half)
        v_pad = [keep_half(vd, lo_kv, e) for e in range(per_tile)]
        p2s = []
        for pair in range(B_GROUP // per_tile):
            h0 = kh * B_GROUP + pair * per_tile
            for e in range(per_tile):
                head = h0 + e
                slope = 2.0 ** (-8.0 * (head + 1) / B_HEADS)
                s2 = s2s[pair][e]
                s = jnp.where(upper, s2[:, :W], s2[:, W:]) - slope * dist
                if sb == 0:
                    s = jnp.where(no_prev, -jnp.inf, s)
                sink = sink_ref[head]
                m = jnp.maximum(jnp.max(s, axis=-1, keepdims=True), sink)
                p = jnp.exp(s - m)
                denom = jnp.sum(p, axis=-1, keepdims=True) + jnp.exp(sink - m)
                p = p * (1.0 / denom)
                p2s.append(jnp.concatenate([jnp.where(upper, p, 0.0), jnp.where(upper, 0.0, p)],
                                           axis=1).astype(BF16))
        for pair in range(B_GROUP // per_tile):
            h0 = kh * B_GROUP + pair * per_tile
            acc = None
            for e in range(per_tile):
                part = jnp.dot(p2s[pair * per_tile + e], v_pad[e],
                               preferred_element_type=F32)
                acc = part if acc is None else acc + part
            o_ref[r0:r0 + W, h0 * B_DH:h0 * B_DH + LANES] = acc.astype(o_ref.dtype)

    units = [(sb, kh) for sb in range(TQ // W) for kh in range(B_KV_HEADS)]
    ahead = 1
    pending = [scores(unit) for unit in units[:ahead]]
    for u, unit in enumerate(units):
        if u + ahead < len(units):
            pending.append(scores(units[u + ahead]))
        finish(unit, pending.pop(0))


def _swa(q, kv, sinks, *, batch, seq, tq):
    T, NQ = q.shape
    NKV2 = kv.shape[1]
    W = B_WINDOW
    nq = seq // tq
    r = tq // W
    return pl.pallas_call(
        _swa_kernel,
        grid=(batch, nq),
        in_specs=[pl.BlockSpec(memory_space=pltpu.SMEM),
                  pl.BlockSpec((tq, NQ), lambda b, j: (b * nq + j, 0)),
                  pl.BlockSpec((W, NKV2), lambda b, j: (jnp.maximum((b * nq + j) * r - 1, 0), 0)),
                  pl.BlockSpec((tq, NKV2), lambda b, j: (b * nq + j, 0))],
        out_specs=pl.BlockSpec((tq, NQ), lambda b, j: (b * nq + j, 0)),
        out_shape=jax.ShapeDtypeStruct((T, NQ), BF16),
        compiler_params=_params(2),
        name="swa",
    )(sinks, q, kv, kv)


def _row(v):
    return v.reshape(1, -1).astype(F32)


def _delta_mixer(h2, norm_g, w_in_all, layer, conv_w, a_log, dt_bias, onorm, *, batch, seq):
    H = A_HEADS
    n_main = 4 * H * A_DK
    w_small = jnp.pad(w_in_all[layer, :, n_main:], ((0, 0), (0, LANES - 2 * H)))
    cw = conv_w.reshape(A_CONV, 3, H, A_DK).transpose(1, 2, 0, 3)
    q, k, v, gate, ps = _in_proj(h2, _row(norm_g), w_in_all, layer, w_small, cw, seq=seq, tm=256)
    arow = jnp.pad(a_log, (H, LANES - 2 * H)).reshape(1, LANES)
    brow = jnp.pad(dt_bias, (H, LANES - 2 * H)).reshape(1, LANES)
    return _delta(q, k, v, gate, ps, arow, brow, _row(onorm), batch=batch, seq=seq, tb=512)


def kernel(x, a_norm, a_w_in, a_conv_w, a_A_log, a_dt_bias, a_onorm, a_w_out, kv_norm, kv_w, kv_b, b_norm, b_w_q, b_b_q, b_sinks, b_w_o, b_b_o, f_norm, f_w_up, f_conv_w, f_conv_b, f_w_down, final_norm):
    B, S, D = x.shape
    depth = f_norm.shape[0]
    n_a = a_norm.shape[0]
    assert n_a >= 1, "the first attention layer takes its kv / q from the preceding ffn call"
    h = x.reshape(B * S, D)
    w_in_all = a_w_in.astype(BF16)
    w_up_all = f_w_up.astype(BF16)
    w_down_all = f_w_down.astype(BF16)
    kv = q = None
    for layer in range(depth):
        if layer < n_a:
            i = layer
            o = _delta_mixer(h, a_norm[i], w_in_all, i, a_conv_w[i], a_A_log[i], a_dt_bias[i],
                             a_onorm[i], batch=B, seq=S)
            w_o, b_o = a_w_out[i], jnp.zeros((1, D), F32)
        else:
            i = layer - n_a
            o = _swa(q, kv, b_sinks[i].astype(F32), batch=B, seq=S, tq=512)
            w_o, b_o = b_w_o[i], _row(b_b_o[i])
        qkv = None
        if n_a <= layer + 1 < depth:
            j = layer + 1 - n_a
            qkv = (_row(kv_norm), kv_w.astype(BF16), _row(kv_b),
                   _row(b_norm[j]), b_w_q[j].astype(BF16), _row(b_b_q[j]))
        res = _ffn(h, o, w_o.astype(BF16), b_o, _row(f_norm[layer]), w_up_all,
                   f_conv_w[layer], _row(f_conv_b[layer]), w_down_all,
                   _row(final_norm), qkv, layer=layer, seq=S, tm=512,
                   final_norm=(layer == depth - 1))
        h = res[0]
        if qkv is not None:
            q = res[2]
            if layer + 1 == n_a:
                kv = res[1]
    return h.reshape(B, S, D)
```

```python
import functools

import jax
import jax.numpy as jnp
from jax import lax
from jax.experimental import pallas as pl
from jax.experimental.pallas import tpu as pltpu

F32 = jnp.float32
BF16 = jnp.bfloat16
EPS = 1e-6

LANES = 128
SUBLANES = 8
VMEM_LIMIT = 56 * 1024 * 1024

A_HEADS = 8
A_DK = 128
A_CONV = 4
A_CHUNK = 64

B_HEADS = 16
B_KV_HEADS = 4
B_GROUP = B_HEADS // B_KV_HEADS
B_DH = 64
B_WINDOW = 128

FFN_CONV = 3


def _params(n_axes, vmem=VMEM_LIMIT):
    return pltpu.CompilerParams(dimension_semantics=("arbitrary",) * n_axes,
                                vmem_limit_bytes=vmem)


def _const_spec(shape):
    nd = len(shape)
    return pl.BlockSpec(shape, lambda *_: (0,) * nd, pipeline_mode=pl.Buffered(1))


def _layer_spec(stacked, layer):
    shape = stacked.shape[1:]
    nd = len(shape)
    return pl.BlockSpec((None,) + shape, lambda *_: (layer,) + (0,) * nd,
                        pipeline_mode=pl.Buffered(1))


def _rms_scale(x):
    return lax.rsqrt(jnp.mean(x * x, axis=-1, keepdims=True) + EPS)


def _sigmoid(x):
    return 1.0 / (1.0 + jnp.exp(-x))


def _silu(x):
    h = 0.5 * x
    return h * jnp.tanh(h) + h


def _softplus(x):
    return jnp.maximum(x, 0.0) + jnp.log(1.0 + jnp.exp(-jnp.abs(x)))


def _in_proj_kernel(x_ref, g_ref, w_ref, ws_ref, cw_ref, q_ref, k_ref, v_ref, gt_ref, ps_ref,
                    xp_scr, *, n_chunk, tiles_per_seq):
    i = pl.program_id(0)
    tm = x_ref.shape[0]
    H, _, DK = q_ref.shape
    per = n_chunk // LANES

    @pl.when(i % tiles_per_seq == 0)
    def _():
        xp_scr[:, 0:SUBLANES, :] = jnp.zeros((xp_scr.shape[0], SUBLANES, LANES), F32)

    x = x_ref[...]
    hn = (x * _rms_scale(x) * g_ref[...]).astype(BF16)
    def project(c):
        res = jnp.dot(hn, w_ref[:, c * n_chunk:(c + 1) * n_chunk], preferred_element_type=F32)
        for j in range(per):
            sec, h = divmod(c * per + j, H)
            u = res[:, j * LANES:(j + 1) * LANES]
            if sec == 3:
                gt_ref[h] = _silu(u).astype(gt_ref.dtype)
            else:
                xp_scr[c * per + j, SUBLANES:SUBLANES + tm, :] = u

    def finish(c):
        for j in range(per):
            slab = c * per + j
            sec, h = divmod(slab, H)
            if sec == 3:
                continue
            u = xp_scr[slab, SUBLANES:SUBLANES + tm, :]
            w = cw_ref[sec, h]
            y = u * w[A_CONV - 1:A_CONV, :]
            for t in range(A_CONV - 1):
                off = SUBLANES - (A_CONV - 1) + t
                y = y + xp_scr[slab, off:off + tm, :] * w[t:t + 1, :]
            xp_scr[slab, 0:SUBLANES, :] = u[tm - SUBLANES:, :]
            y = _silu(y)
            if sec == 0:
                scale = lax.rsqrt(jnp.sum(y * y, axis=-1, keepdims=True) + EPS) * (DK ** -0.5)
                q_ref[h] = (y * scale).astype(q_ref.dtype)
            elif sec == 1:
                scale = lax.rsqrt(jnp.sum(y * y, axis=-1, keepdims=True) + EPS)
                k_ref[h] = (y * scale).astype(k_ref.dtype)
            else:
                v_ref[h] = y.astype(v_ref.dtype)

    n_chunks = 4 * H // per
    project(0)
    for c in range(n_chunks):
        if c + 1 < n_chunks:
            project(c + 1)
        finish(c)
    ps_ref[...] = jnp.dot(hn, ws_ref[...], preferred_element_type=F32)


def _in_proj(x2, g, w_all, layer, w_small, cw, *, seq, tm):
    T, D = x2.shape
    H = A_HEADS
    head_major = pl.BlockSpec((H, tm, A_DK), lambda i: (0, i, 0))
    return pl.pallas_call(
        functools.partial(_in_proj_kernel, n_chunk=512, tiles_per_seq=seq // tm),
        grid=(T // tm,),
        in_specs=[pl.BlockSpec((tm, D), lambda i: (i, 0)),
                  _const_spec((1, D)),
                  _layer_spec(w_all, layer),
                  _const_spec((D, LANES)),
                  _const_spec((3, H, A_CONV, A_DK))],
        out_specs=[head_major, head_major, head_major, head_major,
                   pl.BlockSpec((tm, LANES), lambda i: (i, 0))],
        out_shape=[jax.ShapeDtypeStruct((H, T, A_DK), BF16),
                   jax.ShapeDtypeStruct((H, T, A_DK), BF16),
                   jax.ShapeDtypeStruct((H, T, A_DK), BF16),
                   jax.ShapeDtypeStruct((H, T, A_DK), BF16),
                   jax.ShapeDtypeStruct((T, LANES), F32)],
        scratch_shapes=[pltpu.VMEM((3 * H, tm + SUBLANES, LANES), F32)],
        compiler_params=_params(1),
        name="in_proj",
    )(x2, g, w_all, w_small, cw)


def _unit_lower_inverses(lmats):
    n = lmats[0].shape[0]
    n_tiles = n // SUBLANES
    lane = lax.broadcasted_iota(jnp.int32, (SUBLANES, n), 1)
    sub = lax.broadcasted_iota(jnp.int32, (SUBLANES, n), 0)
    blk = lane // SUBLANES
    base = blk * SUBLANES
    eye = jnp.where(lane - base == sub, 1.0, 0.0).astype(F32)
    zero = jnp.zeros((SUBLANES, n), F32)
    lts = [[lm[t * SUBLANES:(t + 1) * SUBLANES, :] for t in range(n_tiles)] for lm in lmats]
    tts = []
    for lt in lts:
        ld = lt[n_tiles - 1]
        for t in reversed(range(n_tiles - 1)):
            ld = jnp.where(blk == t, lt[t], ld)
        td = eye
        for m in range(SUBLANES - 1):
            col = jnp.take_along_axis(ld, base + m, axis=1)
            td = td - col * td[m:m + 1, :]
        tts.append([jnp.where(blk == t, td, 0.0) for t in range(n_tiles)])
    s = SUBLANES
    while s < n:
        odd = [t for t in range(n_tiles) if (t * SUBLANES // s) % 2 == 1]
        ws = []
        for lt, tt in zip(lts, tts):
            l_odd = jnp.concatenate(
                [jnp.where(lane // s == t * SUBLANES // s - 1, lt[t], 0.0) for t in odd], axis=0)
            t_full = jnp.concatenate(tt, axis=0).astype(BF16)
            ws.append(jnp.dot(l_odd.astype(BF16), t_full, preferred_element_type=F32))
        xs = []
        for tt, w in zip(tts, ws):
            w_rows = {t: w[i * SUBLANES:(i + 1) * SUBLANES, :] for i, t in enumerate(odd)}
            w_full = jnp.concatenate([w_rows.get(t, zero) for t in range(n_tiles)], axis=0)
            t_odd = jnp.concatenate([tt[t] for t in odd], axis=0).astype(BF16)
            xs.append(jnp.dot(t_odd, w_full.astype(BF16), preferred_element_type=F32))
        for tt, x in zip(tts, xs):
            for i, t in enumerate(odd):
                tt[t] = tt[t] - x[i * SUBLANES:(i + 1) * SUBLANES, :]
        s *= 2
    return [jnp.concatenate(tt, axis=0).astype(BF16) for tt in tts]


def _delta_kernel(q_ref, k_ref, v_ref, gt_ref, ps_ref, arow_ref, brow_ref, on_ref,
                  o_ref,
                  state_scr, kq_scr, kt_scr, gc_scr, beta_scr, grow_scr,
                  attn_scr, rhs_scr, val_scr, egc_scr, dl_scr, gam_scr, o_scr, *, group):
    s = pl.program_id(1)
    H, Tb, DK = q_ref.shape
    C = A_CHUNK
    n_chunks = Tb // C

    @pl.when(s == 0)
    def _():
        state_scr[...] = jnp.zeros_like(state_scr)

    for h in range(H):
        qb = q_ref[h]
        kb = k_ref[h]
        ktb = kb.astype(F32).T.astype(BF16)
        for c in range(n_chunks):
            kq_scr[h, c, 0:C, :] = kb[c * C:(c + 1) * C, :]
            kq_scr[h, c, C:2 * C, :] = qb[c * C:(c + 1) * C, :]
            kt_scr[h, c] = ktb[:, c * C:(c + 1) * C]

    ps = ps_ref[...]
    beta_scr[...] = _sigmoid(ps)
    g = -jnp.exp(arow_ref[...]) * _softplus(ps + brow_ref[...])
    pos = lax.broadcasted_iota(jnp.int32, (Tb, LANES), 0) % C
    gc = g
    sh = 1
    while sh < C:
        gc = gc + jnp.where(pos >= sh, pltpu.roll(gc, sh, 0), 0.0)
        sh *= 2
    gc_scr[...] = gc
    gct = gc.T
    for c in range(n_chunks):
        grow_scr[c] = gct[:, c * C:(c + 1) * C]

    ri = lax.broadcasted_iota(jnp.int32, (C, C), 0)
    ci = lax.broadcasted_iota(jnp.int32, (C, C), 1)
    incl = ri >= ci
    strict = ri > ci

    def gate_cols(c, h):
        gcol = jnp.broadcast_to(gc_scr[c * C:(c + 1) * C, H + h:H + h + 1], (C, DK))
        bcol = jnp.broadcast_to(beta_scr[c * C:(c + 1) * C, h:h + 1], (C, DK))
        return gcol, bcol

    for c0 in range(0, n_chunks, group):
        chains = [(c, h) for c in range(c0, c0 + group) for h in range(H)]
        scs = [jnp.dot(kq_scr[h, c], kt_scr[h, c], preferred_element_type=F32)
               for c, h in chains]
        lmats = []
        for (c, h), sc in zip(chains, scs):
            gcol, bcol = gate_cols(c, h)
            grow = grow_scr[c, H + h:H + h + 1, :]
            decay = jnp.exp(jnp.where(incl, gcol[:, :C] - grow, -jnp.inf))
            lmats.append(jnp.where(strict, sc[:C] * (decay * bcol[:, :C]), 0.0))
            attn_scr[h, c] = (sc[C:] * decay).astype(BF16)
            glast = gcol[C - 1:C, :]
            egc = jnp.exp(gcol)
            k = k_ref[h, c * C:(c + 1) * C, :].astype(F32)
            v = v_ref[h, c * C:(c + 1) * C, :].astype(F32)
            rhs_scr[h, c] = jnp.concatenate([k * (bcol * egc), v * bcol], axis=1).astype(BF16)
            egc_scr[h, c] = egc
            dl_scr[h, c] = jnp.exp(glast - gcol)
            gam_scr[h, c] = jnp.broadcast_to(jnp.exp(glast), (SUBLANES, DK))
        tmats = _unit_lower_inverses(lmats)
        sols = [jnp.dot(tmat, rhs_scr[h, c], preferred_element_type=F32)
                for (c, h), tmat in zip(chains, tmats)]
        for (c, h), sol in zip(chains, sols):
            kq_scr[h, c, 0:C, :] = sol[:, :DK].astype(BF16)
            val_scr[h, c] = sol[:, DK:]

    for c in range(n_chunks):
        sts = [state_scr[h] for h in range(H)]
        prs = [jnp.dot(kq_scr[h, c], sts[h].astype(BF16), preferred_element_type=F32)
               for h in range(H)]
        v_news = [val_scr[h, c] - prs[h][:C] for h in range(H)]
        ats = [jnp.dot(attn_scr[h, c], v_news[h].astype(BF16), preferred_element_type=F32)
               for h in range(H)]
        kvs = [jnp.dot(kt_scr[h, c], (dl_scr[h, c] * v_news[h]).astype(BF16),
                       preferred_element_type=F32) for h in range(H)]
        for h in range(H):
            state_scr[h] = sts[h] * gam_scr[h, c][0:1, :] + kvs[h]
            o_scr[h, c * C:(c + 1) * C, :] = egc_scr[h, c] * prs[h][C:] + ats[h]

    for h in range(H):
        o = o_scr[h]
        gate = gt_ref[h]
        y = o * _rms_scale(o) * on_ref[...] * gate
        o_ref[:, h * DK:(h + 1) * DK] = y.astype(o_ref.dtype)


def _delta(q, k, v, gate, ps, arow, brow, onorm, *, batch, seq, tb):
    H = A_HEADS
    DK = A_DK
    T = batch * seq
    ns = seq // tb
    n_chunks = tb // A_CHUNK
    head_major = pl.BlockSpec((H, tb, DK), lambda b, s: (0, b * ns + s, 0))

    return pl.pallas_call(
        functools.partial(_delta_kernel, group=n_chunks),
        grid=(batch, ns),
        in_specs=[head_major, head_major, head_major, head_major,
                  pl.BlockSpec((tb, LANES), lambda b, s: (b * ns + s, 0)),
                  _const_spec((1, LANES)), _const_spec((1, LANES)), _const_spec((1, DK))],
        out_specs=pl.BlockSpec((tb, H * DK), lambda b, s: (b * ns + s, 0)),
        out_shape=jax.ShapeDtypeStruct((T, H * DK), BF16),
        scratch_shapes=[
            pltpu.VMEM((H, DK, DK), F32),
            pltpu.VMEM((H, n_chunks, 2 * A_CHUNK, DK), BF16),
            pltpu.VMEM((H, n_chunks, DK, A_CHUNK), BF16),
            pltpu.VMEM((tb, LANES), F32),
            pltpu.VMEM((tb, LANES), F32),
            pltpu.VMEM((n_chunks, LANES, A_CHUNK), F32),
            pltpu.VMEM((H, n_chunks, A_CHUNK, A_CHUNK), BF16),
            pltpu.VMEM((H, n_chunks, A_CHUNK, 2 * DK), BF16),
            pltpu.VMEM((H, n_chunks, A_CHUNK, DK), F32),
            pltpu.VMEM((H, n_chunks, A_CHUNK, DK), F32),
            pltpu.VMEM((H, n_chunks, A_CHUNK, DK), F32),
            pltpu.VMEM((H, n_chunks, SUBLANES, DK), F32),
            pltpu.VMEM((H, tb, DK), F32),
        ],
        compiler_params=_params(2),
        name="delta",
    )(q, k, v, gate, ps, arow, brow, onorm)


def _ffn_kernel(x_ref, o_ref, wo_ref, bo_ref, g_ref, wu_ref, cw_ref, cb_ref, wd_ref, fg_ref,
                *rest, tiles_per_seq, n_chunk, final_norm, with_qkv):
    if with_qkv:
        (gkv_ref, wkv_ref, bkv_ref, gq_ref, wq_ref, bq_ref,
         out_ref, kv_ref, q_ref, halo_scr, up_scr, act_scr) = rest
    else:
        out_ref, halo_scr, up_scr, act_scr = rest
    i = pl.program_id(0)
    tm = x_ref.shape[0]
    F = wd_ref.shape[0]

    @pl.when(i % tiles_per_seq == 0)
    def _():
        halo_scr[...] = jnp.zeros_like(halo_scr)

    x = (x_ref[...] + bo_ref[...]
         + jnp.dot(o_ref[...], wo_ref[...], preferred_element_type=F32))
    hn = (x * _rms_scale(x) * g_ref[...]).astype(BF16)

    def conv_cols(c0):
        u = jnp.dot(hn, wu_ref[:, c0:c0 + n_chunk], preferred_element_type=F32)
        up_scr[0:SUBLANES, :] = halo_scr[:, c0:c0 + n_chunk]
        up_scr[SUBLANES:SUBLANES + tm, :] = u
        halo_scr[:, c0:c0 + n_chunk] = u[tm - SUBLANES:, :]
        w = cw_ref[:, c0:c0 + n_chunk]
        y = u * w[FFN_CONV - 1:FFN_CONV, :] + cb_ref[:, c0:c0 + n_chunk]
        for j in range(FFN_CONV - 1):
            off = SUBLANES - (FFN_CONV - 1) + j
            y = y + up_scr[off:off + tm, :] * w[j:j + 1, :]
        return y

    for c in range(F // n_chunk):
        gate = conv_cols(c * n_chunk)
        up = conv_cols(F + c * n_chunk)
        act_scr[:, c * n_chunk:(c + 1) * n_chunk] = (_silu(gate) * up).astype(BF16)

    out = x + jnp.dot(act_scr[...], wd_ref[...], preferred_element_type=F32)
    if with_qkv:
        xn = out * _rms_scale(out)
        kv = jnp.dot((xn * gkv_ref[...]).astype(BF16), wkv_ref[...], preferred_element_type=F32)
        kv_ref[...] = (kv + bkv_ref[...]).astype(kv_ref.dtype)
        q = jnp.dot((xn * gq_ref[...]).astype(BF16), wq_ref[...], preferred_element_type=F32)
        q_ref[...] = ((q + bq_ref[...]) * (B_DH ** -0.5)).astype(q_ref.dtype)
    if final_norm:
        out = out * _rms_scale(out) * fg_ref[...]
    out_ref[...] = out


def _ffn(h2, o, w_o, b_o, g, w_up, conv_w, conv_b, w_down, final_g, qkv, *, layer, seq, tm,
         final_norm):
    T, D = h2.shape
    F = w_down.shape[1]
    KO = o.shape[1]
    n_chunk = 256
    row_spec = pl.BlockSpec((tm, D), lambda i: (i, 0))
    in_specs = [row_spec,
                pl.BlockSpec((tm, KO), lambda i: (i, 0)),
                _const_spec((KO, D)),
                _const_spec((1, D)),
                _const_spec((1, D)),
                _layer_spec(w_up, layer),
                _const_spec((FFN_CONV, 2 * F)),
                _const_spec((1, 2 * F)),
                _layer_spec(w_down, layer),
                _const_spec((1, D))]
    out_specs = [row_spec]
    out_shape = [jax.ShapeDtypeStruct((T, D), F32)]
    args = [h2, o, w_o, b_o, g, w_up, conv_w, conv_b, w_down, final_g]
    if qkv is not None:
        nkv, nq = qkv[1].shape[1], qkv[4].shape[1]
        in_specs += [_const_spec(a.shape) for a in qkv]
        out_specs += [pl.BlockSpec((tm, nkv), lambda i: (i, 0)),
                      pl.BlockSpec((tm, nq), lambda i: (i, 0))]
        out_shape += [jax.ShapeDtypeStruct((T, nkv), BF16), jax.ShapeDtypeStruct((T, nq), BF16)]
        args += list(qkv)
    return pl.pallas_call(
        functools.partial(_ffn_kernel, tiles_per_seq=seq // tm, n_chunk=n_chunk,
                          final_norm=final_norm, with_qkv=qkv is not None),
        grid=(T // tm,),
        in_specs=in_specs,
        out_specs=out_specs,
        out_shape=out_shape,
        scratch_shapes=[
            pltpu.VMEM((SUBLANES, 2 * F), F32),
            pltpu.VMEM((tm + SUBLANES, n_chunk), F32),
            pltpu.VMEM((tm, F), BF16),
        ],
        compiler_params=_params(1),
        name="ffn",
    )(*args)


def _swa_kernel(sink_ref, q_ref, kvp_ref, kvc_ref, o_ref):
    j = pl.program_id(1)
    W = B_WINDOW
    TQ = q_ref.shape[0]
    nkv = B_KV_HEADS * B_DH
    per_tile = LANES // B_DH
    ri = lax.broadcasted_iota(jnp.int32, (W, W), 0)
    ci = lax.broadcasted_iota(jnp.int32, (W, W), 1)
    upper = ci > ri
    dist = jnp.where(upper, ri - ci + W, ri - ci).astype(F32)
    no_prev = jnp.logical_and(upper, j == 0)
    assert per_tile == 2
    nt = (((1,), (1,)), ((), ()))

    def window(r0, col):
        cur = kvc_ref[r0:r0 + W, col:col + LANES]
        prev = kvp_ref[:, col:col + LANES] if r0 == 0 else kvc_ref[r0 - W:r0, col:col + LANES]
        return jnp.concatenate([prev, cur], axis=0)

    lo_kv = lax.broadcasted_iota(jnp.int32, (2 * W, LANES), 1) < B_DH
    lo_q = lax.broadcasted_iota(jnp.int32, (W, LANES), 1) < B_DH

    def both_halves(x, half):
        swapped = jnp.concatenate([x[:, B_DH:], x[:, :B_DH]], axis=1)
        return jnp.where(lo_kv, x, swapped) if half == 0 else jnp.where(lo_kv, swapped, x)

    def keep_half(x, lo_mask, e):
        return jnp.where(lo_mask, x, jnp.zeros_like(x)) if e == 0 else jnp.where(lo_mask, jnp.zeros_like(x), x)

    def scores(unit):
        sb, kh = unit
        r0 = sb * W
        tile, half = divmod(kh, per_tile)
        kd = both_halves(window(r0, tile * LANES), half)
        out = []
        for pair in range(B_GROUP // per_tile):
            h0 = kh * B_GROUP + pair * per_tile
            qp = q_ref[r0:r0 + W, h0 * B_DH:h0 * B_DH + LANES]
            out.append([lax.dot_general(keep_half(qp, lo_q, e), kd, nt, preferred_element_type=F32)
                        for e in range(per_tile)])
        return out

    def finish(unit, s2s):
        sb, kh = unit
        r0 = sb * W
        tile, half = divmod(kh, per_tile)
        vd = both_halves(window(r0, nkv + tile * LANES), half)
        v_pad = [keep_half(vd, lo_kv, e) for e in range(per_tile)]
        p2s = []
        for pair in range(B_GROUP // per_tile):
            h0 = kh * B_GROUP + pair * per_tile
            for e in range(per_tile):
                head = h0 + e
                slope = 2.0 ** (-8.0 * (head + 1) / B_HEADS)
                s2 = s2s[pair][e]
                s = jnp.where(upper, s2[:, :W], s2[:, W:]) - slope * dist
                if sb == 0:
                    s = jnp.where(no_prev, -jnp.inf, s)
                sink = sink_ref[head]
                m = jnp.maximum(jnp.max(s, axis=-1, keepdims=True), sink)
                p = jnp.exp(s - m)
                denom = jnp.sum(p, axis=-1, keepdims=True) + jnp.exp(sink - m)
                p = p * (1.0 / denom)
                p2s.append(jnp.concatenate([jnp.where(upper, p, 0.0), jnp.where(upper, 0.0, p)],
                                           axis=1).astype(BF16))
        for pair in range(B_GROUP // per_tile):
            h0 = kh * B_GROUP + pair * per_tile
            acc = None
            for e in range(per_tile):
                part = jnp.dot(p2s[pair * per_tile + e], v_pad[e],
                               preferred_element_type=F32)
                acc = part if acc is None else acc + part
            o_ref[r0:r0 + W, h0 * B_DH:h0 * B_DH + LANES] = acc.astype(o_ref.dtype)

    units = [(sb, kh) for sb in range(TQ // W) for kh in range(B_KV_HEADS)]
    ahead = 1
    pending = [scores(unit) for unit in units[:ahead]]
    for u, unit in enumerate(units):
        if u + ahead < len(units):
            pending.append(scores(units[u + ahead]))
        finish(unit, pending.pop(0))


def _swa(q, kv, sinks, *, batch, seq, tq):
    T, NQ = q.shape
    NKV2 = kv.shape[1]
    W = B_WINDOW
    nq = seq // tq
    r = tq // W
    return pl.pallas_call(
        _swa_kernel,
        grid=(batch, nq),
        in_specs=[pl.BlockSpec(memory_space=pltpu.SMEM),
                  pl.BlockSpec((tq, NQ), lambda b, j: (b * nq + j, 0)),
                  pl.BlockSpec((W, NKV2), lambda b, j: (jnp.maximum((b * nq + j) * r - 1, 0), 0)),
                  pl.BlockSpec((tq, NKV2), lambda b, j: (b * nq + j, 0))],
        out_specs=pl.BlockSpec((tq, NQ), lambda b, j: (b * nq + j, 0)),
        out_shape=jax.ShapeDtypeStruct((T, NQ), BF16),
        compiler_params=_params(2),
        name="swa",
    )(sinks, q, kv, kv)


def _row(v):
    return v.reshape(1, -1).astype(F32)


def _delta_mixer(h2, norm_g, w_in_all, layer, conv_w, a_log, dt_bias, onorm, *, batch, seq):
    H = A_HEADS
    n_main = 4 * H * A_DK
    w_small = jnp.pad(w_in_all[layer, :, n_main:], ((0, 0), (0, LANES - 2 * H)))
    cw = conv_w.reshape(A_CONV, 3, H, A_DK).transpose(1, 2, 0, 3)
    q, k, v, gate, ps = _in_proj(h2, _row(norm_g), w_in_all, layer, w_small, cw, seq=seq, tm=256)
    arow = jnp.pad(a_log, (H, LANES - 2 * H)).reshape(1, LANES)
    brow = jnp.pad(dt_bias, (H, LANES - 2 * H)).reshape(1, LANES)
    return _delta(q, k, v, gate, ps, arow, brow, _row(onorm), batch=batch, seq=seq, tb=512)


def kernel(x, a_norm, a_w_in, a_conv_w, a_A_log, a_dt_bias, a_onorm, a_w_out, kv_norm, kv_w, kv_b, b_norm, b_w_q, b_b_q, b_sinks, b_w_o, b_b_o, f_norm, f_w_up, f_conv_w, f_conv_b, f_w_down, final_norm):
    B, S, D = x.shape
    depth = f_norm.shape[0]
    n_a = a_norm.shape[0]
    assert n_a >= 1, "the first attention layer takes its kv / q from the preceding ffn call"
    h = x.reshape(B * S, D)
    w_in_all = a_w_in.astype(BF16)
    w_up_all = f_w_up.astype(BF16)
    w_down_all = f_w_down.astype(BF16)
    kv = q = None
    for layer in range(depth):
        if layer < n_a:
            i = layer
            o = _delta_mixer(h, a_norm[i], w_in_all, i, a_conv_w[i], a_A_log[i], a_dt_bias[i],
                             a_onorm[i], batch=B, seq=S)
            w_o, b_o = a_w_out[i], jnp.zeros((1, D), F32)
        else:
            i = layer - n_a
            o = _swa(q, kv, b_sinks[i].astype(F32), batch=B, seq=S, tq=512)
            w_o, b_o = b_w_o[i], _row(b_b_o[i])
        qkv = None
        if n_a <= layer + 1 < depth:
            j = layer + 1 - n_a
            qkv = (_row(kv_norm), kv_w.astype(BF16), _row(kv_b),
                   _row(b_norm[j]), b_w_q[j].astype(BF16), _row(b_b_q[j]))
        res = _ffn(h, o, w_o.astype(BF16), b_o, _row(f_norm[layer]), w_up_all,
                   f_conv_w[layer], _row(f_conv_b[layer]), w_down_all,
                   _row(final_norm), qkv, layer=layer, seq=S, tm=512,
                   final_norm=(layer == depth - 1))
        h = res[0]
        if qkv is not None:
            q = res[2]
            if layer + 1 == n_a:
                kv = res[1]
    return h.reshape(B, S, D)
```
